```python
import math
import jax
import jax.numpy as jnp
from jax import lax
import numpy as np

D_MODEL = 1024
BATCH = 16
SEQ = 2048
DEPTH = 2
DEC_BATCH = 32
DEC_SEQ = 1
PAST_LEN = 16384
PAGE_SIZE = 128

N_MIXERS = 2
N_ML = (DEPTH + 1) // 2
N_DA = DEPTH // 2
D_FF = 2816
EPS = 1e-6

ML_HEADS = 4
ML_DV = D_MODEL // ML_HEADS
ML_DK = ML_DV // 2
ML_QK_W = ML_HEADS * ML_DK
ML_V_W = ML_HEADS * ML_DV
ML_IN_W = 2 * ML_QK_W + 2 * ML_V_W + 2 * ML_HEADS
ML_CHUNK = 64
GATE_SOFTCAP = 15.0

DA_HEAD_DIM = 64
DA_HEADS = D_MODEL // (2 * DA_HEAD_DIM)
DA_SUB = 2 * DA_HEADS
DA_VDIM = 2 * DA_HEAD_DIM
DA_QK_W = DA_SUB * DA_HEAD_DIM
DA_V_W = DA_HEADS * DA_VDIM
ROPE_THETA = 500000.0
ROT_DIM = DA_HEAD_DIM // 4
Q_BLOCK = 128

kernel_name = "macaron_mlstm_diffattn_hybrid_step"


def rms_norm(x, gain):
    x32 = x.astype(jnp.float32)
    y = x32 * lax.rsqrt(jnp.mean(x32 * x32, axis=-1, keepdims=True) + EPS)
    return (y * gain.astype(jnp.float32)).astype(x.dtype)


def swiglu_ffn(x, gain, w_gu, w_down):
    h = rms_norm(x, gain)
    g, u = jnp.split(h @ w_gu, 2, axis=-1)
    return (jax.nn.silu(g) * u) @ w_down


def partial_rope(x, pos):
    half = ROT_DIM // 2
    inv_freq = jnp.power(ROPE_THETA, -jnp.arange(0, ROT_DIM, 2, dtype=jnp.float32) / ROT_DIM)
    ang = pos.astype(jnp.float32)[:, None] * inv_freq[None, :]
    cos = jnp.cos(ang)[:, None, :]
    sin = jnp.sin(ang)[:, None, :]
    x1 = x[..., :half].astype(jnp.float32)
    x2 = x[..., half:ROT_DIM].astype(jnp.float32)
    rot = jnp.concatenate([x1 * cos - x2 * sin, x2 * cos + x1 * sin], axis=-1)
    return jnp.concatenate([rot.astype(x.dtype), x[..., ROT_DIM:]], axis=-1)


def mlstm_chunk_step(carry, inp):
    C, n, m = carry
    q, k, v, li, lf = inp
    c = q.shape[2]
    causal = jnp.tril(jnp.ones((c, c), dtype=bool))
    b = jnp.cumsum(lf, axis=-1)
    a = b + m[..., None]
    D = jnp.where(causal, b[..., :, None] - b[..., None, :] + li[..., None, :], -jnp.inf)
    mt = jnp.maximum(a, jnp.max(D, axis=-1))
    w_inter = jnp.exp(a - mt)
    s = jnp.einsum("bhtd,bhsd->bhts", q, k) * jnp.exp(D - mt[..., None])
    num = w_inter[..., None] * jnp.einsum("bhtd,bhde->bhte", q, C) + jnp.einsum("bhts,bhse->bhte", s, v)
    den = w_inter * jnp.einsum("bhtd,bhd->bht", q, n) + jnp.sum(s, axis=-1)
    h = num / jnp.maximum(jnp.abs(den), jnp.exp(-mt))[..., None]
    bL = b[..., -1]
    g = bL[..., None] - b + li
    m_new = jnp.maximum(bL + m, jnp.max(g, axis=-1))
    decay = jnp.exp(bL + m - m_new)
    wk = jnp.exp(g - m_new[..., None])
    C_new = decay[..., None, None] * C + jnp.einsum("bhs,bhsd,bhse->bhde", wk, k, v)
    n_new = decay[..., None] * n + jnp.einsum("bhs,bhsd->bhd", wk, k)
    return (C_new, n_new, m_new), h


def mlstm_recurrence(q, k, v, li, lf, state):
    B, L, H, _ = q.shape
    c = min(ML_CHUNK, L)
    nc = -(-L // c)
    pad = nc * c - L
    f32 = jnp.float32
    q, k, v = (jnp.swapaxes(t.astype(f32), 1, 2) for t in (q, k, v))
    li = jnp.swapaxes(li.astype(f32), 1, 2)
    lf = jnp.swapaxes(lf.astype(f32), 1, 2)
    if pad:
        pw = ((0, 0), (0, 0), (0, pad), (0, 0))
        q, k, v = jnp.pad(q, pw), jnp.pad(k, pw), jnp.pad(v, pw)
        li = jnp.pad(li, pw[:3], constant_values=-jnp.inf)
        lf = jnp.pad(lf, pw[:3])
    def chunks(t):
        return jnp.moveaxis(t.reshape(B, H, nc, c, *t.shape[3:]), 2, 0)
    xs = (chunks(q), chunks(k), chunks(v), chunks(li), chunks(lf))
    init = tuple(s.astype(f32) for s in state)
    final, hs = lax.scan(mlstm_chunk_step, init, xs)
    hs = jnp.moveaxis(hs, 0, 2).reshape(B, H, nc * c, ML_DV)[:, :, :L]
    return jnp.swapaxes(hs, 1, 2), final


def mlstm_mixer(h, w_in, gate_bias, out_gain, w_out, state):
    B, L, _ = h.shape
    z = h @ w_in
    q = z[..., :ML_QK_W].reshape(B, L, ML_HEADS, ML_DK)
    k = z[..., ML_QK_W:2 * ML_QK_W].reshape(B, L, ML_HEADS, ML_DK) * (ML_DK ** -0.5)
    v = z[..., 2 * ML_QK_W:2 * ML_QK_W + ML_V_W].reshape(B, L, ML_HEADS, ML_DV)
    o = z[..., 2 * ML_QK_W + ML_V_W:2 * ML_QK_W + 2 * ML_V_W]
    g = z[..., 2 * ML_QK_W + 2 * ML_V_W:].astype(jnp.float32) + gate_bias.astype(jnp.float32)
    g = GATE_SOFTCAP * jnp.tanh(g / GATE_SOFTCAP)
    li = g[..., :ML_HEADS]
    lf = jax.nn.log_sigmoid(g[..., ML_HEADS:])
    hs, new_state = mlstm_recurrence(q, k, v, li, lf, state)
    hn = rms_norm(hs, out_gain)
    og = jax.nn.sigmoid(o.astype(jnp.float32)).reshape(B, L, ML_HEADS, ML_DV)
    out = (hn * og).reshape(B, L, ML_V_W).astype(h.dtype) @ w_out
    return out, new_state


def lambda_init(layer):
    return 0.8 - 0.6 * math.exp(-0.3 * layer)


def diff_attend(q, q_pos, segs, lam):
    B, Lq = q.shape[0], q.shape[1]
    scale = DA_HEAD_DIM ** -0.5

    def block(qb, qpb):
        qn = qb.shape[1]
        scores = [jnp.where(kp[None, None, None, :] <= qpb[None, None, :, None],
                            jnp.einsum("bqhd,bkhd->bhqk", qb, kb).astype(jnp.float32) * scale,
                            -jnp.inf) for kb, _, kp in segs]
        p = jax.nn.softmax(jnp.concatenate(scores, axis=-1), axis=-1)
        p = p.reshape(B, DA_HEADS, 2, qn, p.shape[-1])
        a = p[:, :, 0] - lam * p[:, :, 1]
        out = jnp.zeros((B, qn, DA_HEADS, DA_VDIM), jnp.float32)
        off = 0
        for kb, vb, _ in segs:
            nk = kb.shape[1]
            out = out + jnp.einsum("bhqk,bkhe->bqhe", a[..., off:off + nk], vb.astype(jnp.float32))
            off += nk
        return out

    if Lq <= Q_BLOCK:
        return block(q, q_pos)
    nb = -(-Lq // Q_BLOCK)
    pad = nb * Q_BLOCK - Lq
    qp = jnp.pad(q, ((0, 0), (0, pad), (0, 0), (0, 0)))
    pp = jnp.concatenate([q_pos, jnp.full((pad,), q_pos[-1], dtype=q_pos.dtype)])
    qs = jnp.swapaxes(qp.reshape(B, nb, Q_BLOCK, DA_SUB, DA_HEAD_DIM), 0, 1)
    ps = pp.reshape(nb, Q_BLOCK)
    out = lax.map(lambda t: block(t[0], t[1]), (qs, ps))
    return jnp.swapaxes(out, 0, 1).reshape(B, nb * Q_BLOCK, DA_HEADS, DA_VDIM)[:, :Lq]


def diff_attn_mixer(h, pos, past, w_qkv, q_gain, k_gain, lam_p, subln, w_out, lam_init):
    B, L, _ = h.shape
    z = h @ w_qkv
    q = z[..., :DA_QK_W].reshape(B, L, DA_SUB, DA_HEAD_DIM)
    k = z[..., DA_QK_W:2 * DA_QK_W].reshape(B, L, DA_SUB, DA_HEAD_DIM)
    v = z[..., 2 * DA_QK_W:].reshape(B, L, DA_HEADS, DA_VDIM)
    q = partial_rope(rms_norm(q, q_gain), pos)
    k = partial_rope(rms_norm(k, k_gain), pos)
    lp = lam_p.astype(jnp.float32)
    lam = jnp.exp(jnp.sum(lp[0] * lp[1])) - jnp.exp(jnp.sum(lp[2] * lp[3])) + lam_init
    segs = ([] if past is None else [past]) + [(k, v, pos)]
    o = diff_attend(q, pos, segs, lam)
    o = rms_norm(o, subln) * (1.0 - lam_init)
    out = o.reshape(B, L, DA_V_W).astype(h.dtype) @ w_out
    return out, (k, v)


def trunk(x, pos, ml_init, attn_past, w):
    ml_final, new_kv = [], []
    for i in range(DEPTH):
        x = x + 0.5 * swiglu_ffn(x, w["ffn1_norm"][i], w["ffn1_w_gu"][i], w["ffn1_w_down"][i])
        h = rms_norm(x, w["mix_norm"][i])
        j = i // N_MIXERS
        if i % N_MIXERS == 0:
            out, st = mlstm_mixer(h, w["ml_w_in"][j], w["ml_gate_bias"][j], w["ml_out_norm"][j],
                                  w["ml_w_out"][j], ml_init[j])
            ml_final.append(st)
        else:
            out, kv = diff_attn_mixer(h, pos, attn_past[j], w["da_w_qkv"][j], w["da_q_norm"][j],
                                      w["da_k_norm"][j], w["da_lambda"][j], w["da_subln"][j],
                                      w["da_w_out"][j], lambda_init(i))
            new_kv.append(kv)
        x = x + out
        x = x + 0.5 * swiglu_ffn(x, w["ffn2_norm"][i], w["ffn2_w_gu"][i], w["ffn2_w_down"][i])
    return x, ml_final, new_kv


def setup_inputs(seed: int = 0) -> dict:
    key = jax.random.key(seed)
    ks = jax.random.split(key, 28)
    f32 = jnp.float32

    def nrm(k, shape, scale):
        return jax.random.normal(k, shape, f32) * scale

    def gain(k, shape):
        return 1.0 + 0.02 * jax.random.normal(k, shape, f32)

    n_pages = PAST_LEN // PAGE_SIZE
    n_used = DEC_BATCH * n_pages
    n_pool = n_used + n_used // 4
    page_table = jax.random.permutation(ks[0], n_pool)[:n_used].reshape(DEC_BATCH, n_pages).astype(jnp.int32)
    f_bias = jnp.linspace(3.0, 6.0, ML_HEADS, dtype=f32)
    ml_gate_bias = jnp.concatenate([nrm(ks[1], (N_ML, ML_HEADS), 0.1),
                                    f_bias[None, :] + nrm(ks[2], (N_ML, ML_HEADS), 0.01)], axis=-1)
    return {
        "x_prompt": nrm(ks[3], (BATCH, SEQ, D_MODEL), 1.0),
        "x_sample": nrm(ks[4], (DEC_BATCH, DEC_SEQ, D_MODEL), 1.0),
        "state_C": nrm(ks[5], (N_ML, DEC_BATCH, ML_HEADS, ML_DK, ML_DV), ML_DK ** -0.5),
        "state_n": nrm(ks[6], (N_ML, DEC_BATCH, ML_HEADS, ML_DK), 0.5),
        "state_m": nrm(ks[7], (N_ML, DEC_BATCH, ML_HEADS), 1.0),
        "cache_k": nrm(ks[8], (N_DA, n_pool, PAGE_SIZE, DA_SUB, DA_HEAD_DIM), 1.0),
        "cache_v": nrm(ks[9], (N_DA, n_pool, PAGE_SIZE, DA_HEADS, DA_VDIM), 1.0),
        "page_table": page_table,
        "ffn1_norm": gain(ks[10], (DEPTH, D_MODEL)),
        "ffn1_w_gu": nrm(ks[11], (DEPTH, D_MODEL, 2 * D_FF), D_MODEL ** -0.5),
        "ffn1_w_down": nrm(ks[12], (DEPTH, D_FF, D_MODEL), D_FF ** -0.5),
        "mix_norm": gain(ks[13], (DEPTH, D_MODEL)),
        "ffn2_norm": gain(ks[14], (DEPTH, D_MODEL)),
        "ffn2_w_gu": nrm(ks[15], (DEPTH, D_MODEL, 2 * D_FF), D_MODEL ** -0.5),
        "ffn2_w_down": nrm(ks[16], (DEPTH, D_FF, D_MODEL), D_FF ** -0.5),
        "ml_w_in": nrm(ks[17], (N_ML, D_MODEL, ML_IN_W), D_MODEL ** -0.5),
        "ml_gate_bias": ml_gate_bias,
        "ml_out_norm": gain(ks[18], (N_ML, ML_HEADS, ML_DV)),
        "ml_w_out": nrm(ks[19], (N_ML, ML_V_W, D_MODEL), ML_V_W ** -0.5),
        "da_w_qkv": nrm(ks[20], (N_DA, D_MODEL, 2 * DA_QK_W + DA_V_W), D_MODEL ** -0.5),
        "da_q_norm": gain(ks[21], (N_DA, DA_HEAD_DIM)),
        "da_k_norm": gain(ks[22], (N_DA, DA_HEAD_DIM)),
        "da_lambda": nrm(ks[23], (N_DA, 4, DA_HEAD_DIM), 0.1),
        "da_subln": gain(ks[24], (N_DA, DA_VDIM)),
        "da_w_out": nrm(ks[25], (N_DA, DA_V_W, D_MODEL), DA_V_W ** -0.5),
    }


def reference(x_prompt, x_sample, state_C, state_n, state_m, cache_k, cache_v, page_table,
              ffn1_norm, ffn1_w_gu, ffn1_w_down, mix_norm, ffn2_norm, ffn2_w_gu, ffn2_w_down,
              ml_w_in, ml_gate_bias, ml_out_norm, ml_w_out,
              da_w_qkv, da_q_norm, da_k_norm, da_lambda, da_subln, da_w_out):
    w = {"ffn1_norm": ffn1_norm, "ffn1_w_gu": ffn1_w_gu, "ffn1_w_down": ffn1_w_down,
         "mix_norm": mix_norm, "ffn2_norm": ffn2_norm, "ffn2_w_gu": ffn2_w_gu,
         "ffn2_w_down": ffn2_w_down, "ml_w_in": ml_w_in, "ml_gate_bias": ml_gate_bias,
         "ml_out_norm": ml_out_norm, "ml_w_out": ml_w_out, "da_w_qkv": da_w_qkv,
         "da_q_norm": da_q_norm, "da_k_norm": da_k_norm, "da_lambda": da_lambda,
         "da_subln": da_subln, "da_w_out": da_w_out}
    f32 = jnp.float32

    Bp, Lp = x_prompt.shape[0], x_prompt.shape[1]
    pos_p = jnp.arange(Lp, dtype=jnp.int32)
    ml_init_p = [(jnp.zeros((Bp, ML_HEADS, ML_DK, ML_DV), f32),
                  jnp.zeros((Bp, ML_HEADS, ML_DK), f32),
                  jnp.zeros((Bp, ML_HEADS), f32)) for _ in range(N_ML)]
    y_prompt, ml_p, kv_p = trunk(x_prompt, pos_p, ml_init_p, [None] * N_DA, w)

    Bs, Ls = x_sample.shape[0], x_sample.shape[1]
    past_len = page_table.shape[1] * PAGE_SIZE
    pos_s = past_len + jnp.arange(Ls, dtype=jnp.int32)
    ml_init_s = [(state_C[j], state_n[j], state_m[j]) for j in range(N_ML)]
    past_pos = jnp.arange(past_len, dtype=jnp.int32)
    attn_past = [(cache_k[j, page_table].reshape(Bs, past_len, DA_SUB, DA_HEAD_DIM),
                  cache_v[j, page_table].reshape(Bs, past_len, DA_HEADS, DA_VDIM),
                  past_pos) for j in range(N_DA)]
    y_sample, ml_s, kv_s = trunk(x_sample, pos_s, ml_init_s, attn_past, w)

    new_C_p = jnp.stack([s[0] for s in ml_p])
    new_n_p = jnp.stack([s[1] for s in ml_p])
    new_m_p = jnp.stack([s[2] for s in ml_p])
    new_C_s = jnp.stack([s[0] for s in ml_s])
    new_n_s = jnp.stack([s[1] for s in ml_s])
    new_m_s = jnp.stack([s[2] for s in ml_s])
    new_k_p = jnp.stack([kv[0] for kv in kv_p])
    new_v_p = jnp.stack([kv[1] for kv in kv_p])
    new_k_s = jnp.stack([kv[0] for kv in kv_s])
    new_v_s = jnp.stack([kv[1] for kv in kv_s])
    return (y_prompt, y_sample, new_C_p, new_n_p, new_m_p, new_C_s, new_n_s, new_m_s,
            new_k_p, new_v_p, new_k_s, new_v_s)
```

```python
import functools
import math

import jax
import jax.numpy as jnp
from jax import lax
from jax.experimental import pallas as pl
from jax.experimental.pallas import tpu as pltpu

F32 = jnp.float32
BF16 = jnp.bfloat16

EPS = 1e-6
ML_HEADS = 4
ML_DK = 128
ML_DV = 256
ML_QK_W = ML_HEADS * ML_DK
ML_V_W = ML_HEADS * ML_DV
GATE_SOFTCAP = 15.0
DA_HEAD_DIM = 64
DA_HEADS = 8
DA_SUB = 2 * DA_HEADS
DA_VDIM = 128
DA_W = DA_SUB * DA_HEAD_DIM
ROPE_THETA = 500000.0
ROT_DIM = DA_HEAD_DIM // 4
ROT_HALF = ROT_DIM // 2
PAGE_SIZE = 128
N_MIXERS = 2

V7X_LANES = 128
V7X_MXU_DIM = 256
V7X_VMEM_LIMIT_BYTES = 56 * 1024 * 1024

FFN_CHUNK = 256
ML_CHUNK = 128
ATTN_Q_BLOCK = 256
ATTN_KV_BLOCK = 256
DECODE_PAGES_PER_STEP = 8


def _lambda_init(layer):
    return 0.8 - 0.6 * math.exp(-0.3 * layer)


def _const_spec(shape):
    zeros = (0,) * len(shape)
    return pl.BlockSpec(shape, lambda *_: zeros, pipeline_mode=pl.Buffered(1))


def _params(semantics):
    return pltpu.CompilerParams(dimension_semantics=semantics,
                                vmem_limit_bytes=V7X_VMEM_LIMIT_BYTES)


def _row_tile(rows, target):
    t = min(rows, target)
    while rows % t:
        t //= 2
    return t


def _rms_rows(x, gain):
    ms = jnp.mean(x * x, axis=-1, keepdims=True)
    return x * lax.rsqrt(ms + EPS) * gain


def _ffn_kernel(*refs, has_pre, n_chunks):
    if has_pre:
        x_ref, a_ref, wo_ref, g_ref, wg_ref, wu_ref, wd_ref, o_ref, h_scr = refs
        x = x_ref[...] + jnp.dot(a_ref[...], wo_ref[...], preferred_element_type=F32)
    else:
        x_ref, g_ref, wg_ref, wu_ref, wd_ref, o_ref, h_scr = refs
        x = x_ref[...]
    h_scr[...] = _rms_rows(x, g_ref[...]).astype(BF16)
    o_ref[...] = x

    def body(c, carry):
        h = h_scr[...]
        g = jnp.dot(h, wg_ref[c], preferred_element_type=F32)
        u = jnp.dot(h, wu_ref[c], preferred_element_type=F32)
        a = (0.5 * g * jax.nn.sigmoid(g) * u).astype(BF16)
        o_ref[...] += jnp.dot(a, wd_ref[c], preferred_element_type=F32)
        return carry

    lax.fori_loop(0, n_chunks, body, 0)


def _ffn_call(x, gain, wg, wu, wd, pre=None, *, row_target=512):
    rows, d = x.shape
    n_chunks, _, tf = wg.shape
    tm = _row_tile(rows, row_target)
    row_spec = pl.BlockSpec((tm, d), lambda i: (i, 0))
    in_specs, args = [row_spec], [x]
    if pre is not None:
        a, wo = pre
        in_specs += [pl.BlockSpec((tm, a.shape[1]), lambda i: (i, 0)), _const_spec(wo.shape)]
        args += [a, wo]
    in_specs += [_const_spec((1, d)), _const_spec(wg.shape), _const_spec(wu.shape),
                 _const_spec(wd.shape)]
    args += [gain.reshape(1, d), wg, wu, wd]
    return pl.pallas_call(
        functools.partial(_ffn_kernel, has_pre=pre is not None, n_chunks=n_chunks),
        out_shape=jax.ShapeDtypeStruct((rows, d), F32),
        grid=(rows // tm,),
        in_specs=in_specs,
        out_specs=row_spec,
        scratch_shapes=[pltpu.VMEM((tm, d), BF16)],
        compiler_params=_params(("parallel",)),
        name="ffn_pre" if pre is not None else "ffn",
    )(*args)


def _ffn_weights(w_gu, w_down):
    d, two_ff = w_gu.shape
    d_ff = two_ff // 2
    tf = FFN_CHUNK
    while d_ff % tf:
        tf //= 2
    nc = d_ff // tf
    wg = w_gu[:, :d_ff].reshape(d, nc, tf).transpose(1, 0, 2).astype(BF16)
    wu = w_gu[:, d_ff:].reshape(d, nc, tf).transpose(1, 0, 2).astype(BF16)
    wd = w_down.reshape(nc, tf, d).astype(BF16)
    return wg, wu, wd


def _activate_gates(g, is_forget):
    g = GATE_SOFTCAP * jnp.tanh(g / GATE_SOFTCAP)
    log_sig = jnp.minimum(g, 0.0) - jnp.log1p(jnp.exp(-jnp.abs(g)))
    return jnp.where(is_forget, log_sig, g)


def _ml_proj_prompt_kernel(x_ref, g_ref, wqvo_ref, wkT_ref, wgT_ref, gb_ref,
                           q_ref, kT_ref, v_ref, o_ref, gates_ref):
    h = _rms_rows(x_ref[...], g_ref[...]).astype(BF16)
    z = jnp.dot(h, wqvo_ref[...], preferred_element_type=F32)
    q_ref[...] = z[:, :ML_QK_W].astype(BF16)
    v_ref[...] = z[:, ML_QK_W:ML_QK_W + ML_V_W].astype(BF16)
    o_ref[...] = z[:, ML_QK_W + ML_V_W:]
    nt = (((1,), (1,)), ((), ()))
    kT = lax.dot_general(wkT_ref[...], h, nt, preferred_element_type=F32)
    kT_ref[0] = (kT * (ML_DK ** -0.5)).astype(BF16)
    gT = lax.dot_general(wgT_ref[...], h, nt, preferred_element_type=F32) + gb_ref[...]
    row = lax.broadcasted_iota(jnp.int32, gT.shape, 0)
    gates_ref[0] = _activate_gates(gT, row >= ML_HEADS)


def _ml_proj_prompt_call(x, gain, w_in, gate_bias, batch, seq, *, row_target=512):
    rows, d = x.shape
    tm = _row_tile(seq, row_target)
    per_seq = seq // tm
    ng = 2 * ML_HEADS
    wqvo = jnp.concatenate([w_in[:, :ML_QK_W], w_in[:, 2 * ML_QK_W:2 * ML_QK_W + 2 * ML_V_W]],
                           axis=1).astype(BF16)
    wkT = w_in[:, ML_QK_W:2 * ML_QK_W].T.astype(BF16)
    wgT = w_in[:, 2 * ML_QK_W + 2 * ML_V_W:].T.astype(BF16)
    row = lambda w: pl.BlockSpec((tm, w), lambda i: (i, 0))
    by_seq = lambda r: pl.BlockSpec((1, r, tm), lambda i: (i // per_seq, 0, i % per_seq))
    return pl.pallas_call(
        _ml_proj_prompt_kernel,
        out_shape=(jax.ShapeDtypeStruct((rows, ML_QK_W), BF16),
                   jax.ShapeDtypeStruct((batch, ML_QK_W, seq), BF16),
                   jax.ShapeDtypeStruct((rows, ML_V_W), BF16),
                   jax.ShapeDtypeStruct((rows, ML_V_W), F32),
                   jax.ShapeDtypeStruct((batch, ng, seq), F32)),
        grid=(rows // tm,),
        in_specs=[row(d), _const_spec((1, d)), _const_spec(wqvo.shape), _const_spec(wkT.shape),
                  _const_spec(wgT.shape), _const_spec((ng, 1))],
        out_specs=(row(ML_QK_W), by_seq(ML_QK_W), row(ML_V_W), row(ML_V_W), by_seq(ng)),
        compiler_params=_params(("parallel",)),
        name="ml_proj_prompt",
    )(x, gain.reshape(1, d), wqvo, wkT, wgT, gate_bias.reshape(ng, 1).astype(F32))


def _ml_recur_kernel(q_ref, kT_ref, v_ref, o_ref, gates_ref, gain_ref,
                     act_ref, c_out, n_out, m_out, c_s, n_s, m_s, *, chunk):
    j = pl.program_id(1)

    @pl.when(j == 0)
    def _():
        c_s[...] = jnp.zeros_like(c_s)
        n_s[...] = jnp.zeros_like(n_s)
        m_s[...] = jnp.zeros_like(m_s)

    gates = gates_ref[0]
    lane = lax.broadcasted_iota(jnp.int32, gates.shape, 1)
    cum = gates
    step = 1
    while step < chunk:
        cum = cum + jnp.where(lane >= step, pltpu.roll(cum, step, axis=1), 0.0)
        step *= 2
    t_idx = lax.broadcasted_iota(jnp.int32, (chunk, chunk), 0)
    s_idx = lax.broadcasted_iota(jnp.int32, (chunk, chunk), 1)
    causal = s_idx <= t_idx
    nt = (((1,), (1,)), ((), ()))

    for h in range(ML_HEADS):
        li_r = gates[h:h + 1, :]
        lf_r = gates[ML_HEADS + h:ML_HEADS + h + 1, :]
        b_r = cum[ML_HEADS + h:ML_HEADS + h + 1, :]
        b_last = b_r[:, chunk - 1:chunk]
        b_c = jnp.sum(jnp.where(causal, lf_r, 0.0), axis=1, keepdims=True)
        m_prev = m_s[h:h + 1, 0:1]
        n_row = n_s[h:h + 1, :]
        c_old = c_s[h]

        qh = q_ref[0, :, h * ML_DK:(h + 1) * ML_DK]
        kTh = kT_ref[0, h * ML_DK:(h + 1) * ML_DK, :]
        vh = v_ref[0, :, h * ML_DV:(h + 1) * ML_DV]

        d_mat = jnp.where(causal, b_c - b_r + li_r, -jnp.inf)
        a = b_c + m_prev
        mt = jnp.maximum(a, jnp.max(d_mat, axis=1, keepdims=True))
        s = jnp.dot(qh, kTh, preferred_element_type=F32) * jnp.exp(d_mat - mt)
        w_inter = jnp.exp(a - mt)
        num = (w_inter * jnp.dot(qh, c_old.astype(BF16), preferred_element_type=F32)
               + jnp.dot(s.astype(BF16), vh, preferred_element_type=F32))
        den = (w_inter * jnp.sum(qh.astype(F32) * n_row, axis=1, keepdims=True)
               + jnp.sum(s, axis=1, keepdims=True))
        hid = num / jnp.maximum(jnp.abs(den), jnp.exp(-mt))
        hn = _rms_rows(hid, gain_ref[h:h + 1, :])
        og = jax.nn.sigmoid(o_ref[0, :, h * ML_DV:(h + 1) * ML_DV])
        act_ref[0, :, h * ML_DV:(h + 1) * ML_DV] = (hn * og).astype(BF16)

        g_r = b_last - b_r + li_r
        m_new = jnp.maximum(b_last + m_prev, jnp.max(g_r, axis=1, keepdims=True))
        decay = jnp.exp(b_last + m_prev - m_new)
        wk = jnp.exp(g_r - m_new)
        kw = (kTh.astype(F32) * wk).astype(BF16)
        c_s[h] = decay * c_old + jnp.dot(kw, vh, preferred_element_type=F32)
        n_s[h:h + 1, :] = decay * n_row + lax.dot_general(
            wk.astype(BF16), kTh, nt, preferred_element_type=F32)
        m_s[h:h + 1, :] = jnp.broadcast_to(m_new, (1, V7X_LANES))

    @pl.when(j == pl.num_programs(1) - 1)
    def _():
        c_out[0] = c_s[...]
        n_out[0] = n_s[0:ML_HEADS, :]
        lane_m = lax.broadcasted_iota(jnp.int32, (1, V7X_LANES), 1)
        m_row = jnp.zeros((1, V7X_LANES), F32)
        for h in range(ML_HEADS):
            m_row = jnp.where(lane_m == h, m_s[h:h + 1, :], m_row)
        m_out[0] = m_row


def _ml_recur_call(q, kT, v, o, gates, out_gain, batch, seq):
    chunk = _row_tile(seq, ML_CHUNK)
    n_steps = seq // chunk
    ng = 2 * ML_HEADS
    q3 = q.reshape(batch, seq, ML_QK_W)
    v3 = v.reshape(batch, seq, ML_V_W)
    o3 = o.reshape(batch, seq, ML_V_W)
    tok = lambda w: pl.BlockSpec((1, chunk, w), lambda b, j: (b, j, 0))
    col = lambda r: pl.BlockSpec((1, r, chunk), lambda b, j: (b, 0, j))
    per_b = lambda *s: pl.BlockSpec((1,) + s, lambda b, j: (b,) + (0,) * len(s))
    act, c_new, n_new, m_new = pl.pallas_call(
        functools.partial(_ml_recur_kernel, chunk=chunk),
        out_shape=(jax.ShapeDtypeStruct((batch, seq, ML_V_W), BF16),
                   jax.ShapeDtypeStruct((batch, ML_HEADS, ML_DK, ML_DV), F32),
                   jax.ShapeDtypeStruct((batch, ML_HEADS, ML_DK), F32),
                   jax.ShapeDtypeStruct((batch, 1, V7X_LANES), F32)),
        grid=(batch, n_steps),
        in_specs=[tok(ML_QK_W), col(ML_QK_W), tok(ML_V_W), tok(ML_V_W), col(ng),
                  pl.BlockSpec((ML_HEADS, ML_DV), lambda b, j: (0, 0))],
        out_specs=(tok(ML_V_W), per_b(ML_HEADS, ML_DK, ML_DV), per_b(ML_HEADS, ML_DK),
                   per_b(1, V7X_LANES)),
        scratch_shapes=[pltpu.VMEM((ML_HEADS, ML_DK, ML_DV), F32),
                        pltpu.VMEM((8, ML_DK), F32),
                        pltpu.VMEM((8, V7X_LANES), F32)],
        compiler_params=_params(("parallel", "arbitrary")),
        name="ml_recur",
    )(q3, kT, v3, o3, gates, out_gain.astype(F32))
    return act.reshape(batch * seq, ML_V_W), c_new, n_new, m_new[:, 0, :ML_HEADS]


def _ml_proj_decode_kernel(x_ref, g_ref, w_ref, wqkT_ref, wg_ref, gb_ref, z_ref, zT_ref, gates_ref):
    h = _rms_rows(x_ref[...], g_ref[...]).astype(BF16)
    z = jnp.dot(h, w_ref[...], preferred_element_type=F32)
    col = lax.broadcasted_iota(jnp.int32, z.shape, 1)
    is_k = (col >= ML_QK_W) & (col < 2 * ML_QK_W)
    z_ref[...] = jnp.where(is_k, z * (ML_DK ** -0.5), z)
    nt = (((1,), (1,)), ((), ()))
    zT = lax.dot_general(wqkT_ref[...], h, nt, preferred_element_type=F32)
    rowi = lax.broadcasted_iota(jnp.int32, zT.shape, 0)
    zT_ref[...] = jnp.where(rowi >= ML_QK_W, zT * (ML_DK ** -0.5), zT)
    g = jnp.dot(h, wg_ref[...], preferred_element_type=F32) + gb_ref[...]
    lane = lax.broadcasted_iota(jnp.int32, g.shape, 1)
    gates_ref[...] = _activate_gates(g, lane >= ML_HEADS)


def _ml_proj_decode_call(x, gain, w_in, gate_bias):
    rows, d = x.shape
    ng = 2 * ML_HEADS
    w_main = w_in[:, :2 * ML_QK_W + 2 * ML_V_W].astype(BF16)
    wqkT = w_in[:, :2 * ML_QK_W].T.astype(BF16)
    wg = jnp.pad(w_in[:, 2 * ML_QK_W + 2 * ML_V_W:], ((0, 0), (0, V7X_LANES - ng))).astype(BF16)
    gb = jnp.pad(gate_bias.astype(F32), (0, V7X_LANES - ng)).reshape(1, V7X_LANES)
    full = lambda a: _const_spec(a.shape)
    args = (x, gain.reshape(1, d), w_main, wqkT, wg, gb)
    out_shape = (jax.ShapeDtypeStruct((rows, w_main.shape[1]), F32),
                 jax.ShapeDtypeStruct((2 * ML_QK_W, rows), F32),
                 jax.ShapeDtypeStruct((rows, V7X_LANES), F32))
    return pl.pallas_call(
        _ml_proj_decode_kernel,
        out_shape=out_shape,
        grid=(1,),
        in_specs=[full(a) for a in args],
        out_specs=tuple(pl.BlockSpec(s.shape, lambda i: (0, 0)) for s in out_shape),
        compiler_params=_params(("arbitrary",)),
        name="ml_proj_decode",
    )(*args)


def _ml_decode_kernel(z_ref, zT_ref, gates_ref, c_ref, n_ref, m_ref, gain_ref,
                      act_ref, c_out, n_out, m_out):
    b = pl.program_id(0)
    z = z_ref[pl.ds(b, 1), :]
    g = gates_ref[pl.ds(b, 1), :]
    m_in = m_ref[pl.ds(b, 1), :]
    zT = zT_ref[...]
    pick = lax.broadcasted_iota(jnp.int32, zT.shape, 1) == b
    z_col = jnp.sum(jnp.where(pick, zT, 0.0), axis=1, keepdims=True)
    lane_m = lax.broadcasted_iota(jnp.int32, (1, V7X_LANES), 1)
    m_row = jnp.zeros((1, V7X_LANES), F32)

    for h in range(ML_HEADS):
        li = g[:, h:h + 1]
        lf = g[:, ML_HEADS + h:ML_HEADS + h + 1]
        m_prev = m_in[:, h:h + 1]
        q_row = z[:, h * ML_DK:(h + 1) * ML_DK]
        k_row = z[:, ML_QK_W + h * ML_DK:ML_QK_W + (h + 1) * ML_DK]
        v_row = z[:, 2 * ML_QK_W + h * ML_DV:2 * ML_QK_W + (h + 1) * ML_DV]
        o_row = z[:, 2 * ML_QK_W + ML_V_W + h * ML_DV:2 * ML_QK_W + ML_V_W + (h + 1) * ML_DV]
        q_col = z_col[h * ML_DK:(h + 1) * ML_DK, :]
        k_col = z_col[ML_QK_W + h * ML_DK:ML_QK_W + (h + 1) * ML_DK, :]
        c_old = c_ref[0, h]
        n_row = n_ref[0, h:h + 1, :]

        a = lf + m_prev
        mt = jnp.maximum(a, li)
        w_inter = jnp.exp(a - mt)
        s = jnp.sum(q_row * k_row, axis=1, keepdims=True) * jnp.exp(li - mt)
        num = w_inter * jnp.sum(q_col * c_old, axis=0, keepdims=True) + s * v_row
        den = w_inter * jnp.sum(q_row * n_row, axis=1, keepdims=True) + s
        hid = num / jnp.maximum(jnp.abs(den), jnp.exp(-mt))
        hn = _rms_rows(hid, gain_ref[h:h + 1, :])
        act_ref[0, :, h * ML_DV:(h + 1) * ML_DV] = (hn * jax.nn.sigmoid(o_row)).astype(BF16)

        m_new = jnp.maximum(a, li)
        decay = jnp.exp(a - m_new)
        wk = jnp.exp(li - m_new)
        c_out[0, h] = decay * c_old + (wk * k_col) * v_row
        n_out[0, h:h + 1, :] = decay * n_row + wk * k_row
        m_row = jnp.where(lane_m == h, m_new, m_row)
    m_out[0] = m_row


def _ml_decode_call(z, zT, gates, state_c, state_n, state_m, out_gain):
    rows = z.shape[0]
    full = lambda a: _const_spec(a.shape)
    per_b = lambda *s: pl.BlockSpec((1,) + s, lambda b: (b,) + (0,) * len(s))
    act, c_new, n_new, m_new = pl.pallas_call(
        _ml_decode_kernel,
        out_shape=(jax.ShapeDtypeStruct((rows, 1, ML_V_W), BF16),
                   jax.ShapeDtypeStruct(state_c.shape, F32),
                   jax.ShapeDtypeStruct(state_n.shape, F32),
                   jax.ShapeDtypeStruct((rows, 1, V7X_LANES), F32)),
        grid=(rows,),
        in_specs=[full(z), full(zT), full(gates), per_b(ML_HEADS, ML_DK, ML_DV),
                  per_b(ML_HEADS, ML_DK), full(state_m), full(out_gain)],
        out_specs=(per_b(1, ML_V_W), per_b(ML_HEADS, ML_DK, ML_DV), per_b(ML_HEADS, ML_DK),
                   per_b(1, V7X_LANES)),
        compiler_params=_params(("arbitrary",)),
        name="ml_decode",
    )(z, zT, gates, state_c, state_n, state_m, out_gain.astype(F32))
    return act.reshape(rows, ML_V_W), c_new, n_new, m_new[:, 0, :ML_HEADS]


def _subhead_mean_square(t, bsum):
    t2 = t * t
    hi = t2.astype(BF16)
    lo = (t2 - hi.astype(F32)).astype(BF16)
    parts = []
    for j in range(t.shape[1] // V7X_MXU_DIM):
        sl = slice(j * V7X_MXU_DIM, (j + 1) * V7X_MXU_DIM)
        parts.append(jnp.dot(hi[:, sl], bsum, preferred_element_type=F32)
                     + jnp.dot(lo[:, sl], bsum, preferred_element_type=F32))
    return jnp.concatenate(parts, axis=1)


def _rope(t, cos, sin_signed):
    lane = lax.broadcasted_iota(jnp.int32, (t.shape[0], V7X_LANES), 1)
    first = (lane % DA_HEAD_DIM) < ROT_HALF
    parts = []
    for j in range(t.shape[1] // V7X_LANES):
        tj = t[:, j * V7X_LANES:(j + 1) * V7X_LANES]
        partner = jnp.where(first, pltpu.roll(tj, V7X_LANES - ROT_HALF, axis=1),
                            pltpu.roll(tj, ROT_HALF, axis=1))
        parts.append(tj * cos + partner * sin_signed)
    return jnp.concatenate(parts, axis=1)


def _da_proj_kernel(x_ref, g_ref, w_ref, qg_ref, kg_ref, bsum_ref, cos_ref, sin_ref,
                    q_ref, k_ref, kb_ref, v_ref, vb_ref):
    h = _rms_rows(x_ref[...], g_ref[...]).astype(BF16)
    z = jnp.dot(h, w_ref[...], preferred_element_type=F32)
    bsum = bsum_ref[...]
    cos = cos_ref[...]
    sin = sin_ref[...]
    q = z[:, :DA_W]
    k = z[:, DA_W:2 * DA_W]
    v = z[:, 2 * DA_W:]
    q = _rope(q * lax.rsqrt(_subhead_mean_square(q, bsum) + EPS) * qg_ref[...], cos, sin)
    k = _rope(k * lax.rsqrt(_subhead_mean_square(k, bsum) + EPS) * kg_ref[...], cos, sin)
    q_ref[...] = (q * (DA_HEAD_DIM ** -0.5)).astype(BF16)
    k_ref[...] = k
    kb_ref[...] = k.astype(BF16)
    v_ref[...] = v
    vb_ref[...] = v.astype(BF16)


def _rope_tables(pos):
    inv_freq = jnp.power(ROPE_THETA, -jnp.arange(0, ROT_DIM, 2, dtype=F32) / ROT_DIM)
    ang = pos.astype(F32)[:, None] * inv_freq[None, :]
    cos, sin = jnp.cos(ang), jnp.sin(ang)
    ones = jnp.ones((pos.shape[0], DA_HEAD_DIM - ROT_DIM), F32)
    cos64 = jnp.concatenate([cos, cos, ones], axis=1)
    sin64 = jnp.concatenate([-sin, sin, 0.0 * ones], axis=1)
    return jnp.tile(cos64, (1, 2)), jnp.tile(sin64, (1, 2))


def _da_proj_call(x, gain, w_qkv, q_gain, k_gain, pos, seq, *, row_target=512):
    rows, d = x.shape
    tm = _row_tile(seq, row_target)
    per_seq = seq // tm
    w = w_qkv.astype(BF16)
    cos, sin = _rope_tables(pos)
    blk = jnp.arange(V7X_MXU_DIM) // DA_HEAD_DIM
    bsum = jnp.where(blk[:, None] == blk[None, :], 1.0 / DA_HEAD_DIM, 0.0).astype(BF16)
    qg = jnp.tile(q_gain.astype(F32), DA_SUB).reshape(1, DA_W)
    kg = jnp.tile(k_gain.astype(F32), DA_SUB).reshape(1, DA_W)
    row = lambda wd: pl.BlockSpec((tm, wd), lambda i: (i, 0))
    table = pl.BlockSpec((tm, V7X_LANES), lambda i: (i % per_seq, 0))
    n_v = w.shape[1] - 2 * DA_W
    return pl.pallas_call(
        _da_proj_kernel,
        out_shape=(jax.ShapeDtypeStruct((rows, DA_W), BF16),
                   jax.ShapeDtypeStruct((rows, DA_W), F32),
                   jax.ShapeDtypeStruct((rows, DA_W), BF16),
                   jax.ShapeDtypeStruct((rows, n_v), F32),
                   jax.ShapeDtypeStruct((rows, n_v), BF16)),
        grid=(rows // tm,),
        in_specs=[row(d), _const_spec((1, d)), _const_spec(w.shape), _const_spec((1, DA_W)),
                  _const_spec((1, DA_W)), _const_spec(bsum.shape), table, table],
        out_specs=(row(DA_W), row(DA_W), row(DA_W), row(n_v), row(n_v)),
        compiler_params=_params(("parallel",)),
        name="da_proj",
    )(x, gain.reshape(1, d), w, qg, kg, bsum, cos, sin)


def _lambda_value(lam_ref, lam_init):
    lp = lam_ref[...]
    s1 = jnp.sum(lp[0:1, :] * lp[1:2, :], axis=1, keepdims=True)
    s2 = jnp.sum(lp[2:3, :] * lp[3:4, :], axis=1, keepdims=True)
    return jnp.exp(s1) - jnp.exp(s2) + lam_init


def _da_attn_kernel(q_ref, k_ref, v_ref, lam_ref, subln_ref, o_ref, *, tq, tk, lam_init):
    i = pl.program_id(1)
    lam = _lambda_value(lam_ref, lam_init)
    nt = (((1,), (1,)), ((), ()))
    lane = lax.broadcasted_iota(jnp.int32, (tq, V7X_LANES), 1)
    row_g = i * tq + lax.broadcasted_iota(jnp.int32, (2 * tq, tk), 0) % tq
    col_l = lax.broadcasted_iota(jnp.int32, (2 * tq, tk), 1)
    n_full = (i * tq) // tk

    for h in range(DA_HEADS):
        hs = slice(h * DA_VDIM, (h + 1) * DA_VDIM)
        qp = q_ref[0, :, hs]
        zero = jnp.zeros_like(qp)
        q2 = jnp.concatenate([jnp.where(lane < DA_HEAD_DIM, qp, zero),
                              jnp.where(lane >= DA_HEAD_DIM, qp, zero)], axis=0)

        def step(j, carry, masked):
            m, l, acc = carry
            start = pl.multiple_of(j * tk, tk)
            kb = k_ref[0, pl.ds(start, tk), hs]
            vb = v_ref[0, pl.ds(start, tk), hs]
            s = lax.dot_general(q2, kb, nt, preferred_element_type=F32)
            if masked:
                s = jnp.where(j * tk + col_l <= row_g, s, -jnp.inf)
            m_new = jnp.maximum(m, jnp.max(s, axis=1, keepdims=True))
            alpha = jnp.exp(m - m_new)
            p = jnp.exp(s - m_new)
            l = alpha * l + jnp.sum(p, axis=1, keepdims=True)
            acc = alpha * acc + jnp.dot(p.astype(BF16), vb, preferred_element_type=F32)
            return m_new, l, acc

        carry = (jnp.full((2 * tq, 1), -jnp.inf, F32), jnp.zeros((2 * tq, 1), F32),
                 jnp.zeros((2 * tq, DA_VDIM), F32))
        carry = lax.fori_loop(0, n_full, functools.partial(step, masked=False), carry)
        _, l, acc = step(n_full, carry, True)
        o = acc / l
        o = o[:tq] - lam * o[tq:]
        o = _rms_rows(o, subln_ref[...]) * (1.0 - lam_init)
        o_ref[0, :, hs] = o.astype(BF16)


def _da_attn_call(q, kb, vb, lam_p, subln, batch, seq, lam_init):
    tq = _row_tile(seq, ATTN_Q_BLOCK)
    tk = _row_tile(seq, ATTN_KV_BLOCK)
    assert tk % tq == 0 or tq % tk == 0
    if tq > tk:
        tq = tk
    w = q.shape[1]
    q3, k3, v3 = (t.reshape(batch, seq, t.shape[1]) for t in (q, kb, vb))
    whole = lambda wd: pl.BlockSpec((1, seq, wd), lambda b, i: (b, 0, 0))
    out = pl.pallas_call(
        functools.partial(_da_attn_kernel, tq=tq, tk=tk, lam_init=lam_init),
        out_shape=jax.ShapeDtypeStruct((batch, seq, v3.shape[2]), BF16),
        grid=(batch, seq // tq),
        in_specs=[pl.BlockSpec((1, tq, w), lambda b, i: (b, i, 0)), whole(w), whole(v3.shape[2]),
                  pl.BlockSpec(lam_p.shape, lambda b, i: (0, 0)),
                  pl.BlockSpec((1, DA_VDIM), lambda b, i: (0, 0))],
        out_specs=pl.BlockSpec((1, tq, v3.shape[2]), lambda b, i: (b, i, 0)),
        compiler_params=_params(("parallel", "arbitrary")),
        name="da_attn_prompt",
    )(q3, k3, v3, lam_p.astype(F32), subln.reshape(1, DA_VDIM).astype(F32))
    return out.reshape(batch * seq, v3.shape[2])


def _da_decode_kernel(pt_ref, q_ref, kn_ref, vn_ref, lam_ref, subln_ref, *rest, pages, lam_init):
    k_refs = rest[:pages]
    v_refs = rest[pages:2 * pages]
    o_ref, qbd_s, m_s, l_s, acc_s = rest[2 * pages:]
    j = pl.program_id(1)
    nt = (((1,), (1,)), ((), ()))

    @pl.when(j == 0)
    def _():
        q = q_ref[0]
        sub = lax.broadcasted_iota(jnp.int32, (DA_SUB, DA_W), 0)
        col = lax.broadcasted_iota(jnp.int32, (DA_SUB, DA_W), 1)
        qb = jnp.broadcast_to(q.astype(F32), (DA_SUB, DA_W))
        qbd_s[...] = jnp.where(col // DA_HEAD_DIM == sub, qb, 0.0).astype(BF16)
        m_s[...] = jnp.full(m_s.shape, -jnp.inf, F32)
        l_s[...] = jnp.zeros_like(l_s)
        acc_s[...] = jnp.zeros_like(acc_s)

    qbd = qbd_s[...]
    s = jnp.concatenate(
        [lax.dot_general(qbd, k_refs[i][...].astype(BF16), nt, preferred_element_type=F32)
         for i in range(pages)], axis=1)
    m_old = m_s[...]
    m_new = jnp.maximum(m_old, jnp.max(s, axis=1, keepdims=True))
    alpha = jnp.exp(m_old - m_new)
    p = jnp.exp(s - m_new)
    l_s[...] = alpha * l_s[...] + jnp.sum(p, axis=1, keepdims=True)
    pv = jnp.zeros(acc_s.shape, F32)
    for i in range(pages):
        pv += jnp.dot(p[:, i * PAGE_SIZE:(i + 1) * PAGE_SIZE].astype(BF16),
                      v_refs[i][...].astype(BF16), preferred_element_type=F32)
    acc_s[...] = alpha * acc_s[...] + pv
    m_s[...] = m_new

    @pl.when(j == pl.num_programs(1) - 1)
    def _():
        lam = _lambda_value(lam_ref, lam_init)
        m_p = m_s[...]
        s_self = jnp.sum(qbd.astype(F32) * kn_ref[0], axis=1, keepdims=True)
        m_f = jnp.maximum(m_p, s_self)
        a_f = jnp.exp(m_p - m_f)
        p_self = jnp.exp(s_self - m_f)
        l_f = a_f * l_s[...] + p_self
        o = (a_f * acc_s[...] + p_self * vn_ref[0]) / l_f
        for h in range(DA_HEADS):
            hs = slice(h * DA_VDIM, (h + 1) * DA_VDIM)
            oh = o[2 * h:2 * h + 1, hs] - lam * o[2 * h + 1:2 * h + 2, hs]
            oh = _rms_rows(oh, subln_ref[...]) * (1.0 - lam_init)
            o_ref[0, h:h + 1, :] = oh.astype(BF16)


def _da_decode_call(q, k_new, v_new, cache_k, cache_v, page_table, lam_p, subln, lam_init):
    rows = q.shape[0]
    n_pool = cache_k.shape[0]
    n_pages = page_table.shape[1]
    pages = DECODE_PAGES_PER_STEP
    while n_pages % pages:
        pages //= 2
    ck = cache_k.reshape(n_pool, PAGE_SIZE, DA_W)
    cv = cache_v.reshape(n_pool, PAGE_SIZE, DA_HEADS * DA_VDIM)
    per_b = lambda a: pl.BlockSpec((1,) + a.shape[1:], lambda b, j, pt: (b, 0, 0))

    def page_spec(i):
        return pl.BlockSpec((None, PAGE_SIZE, DA_W), lambda b, j, pt: (pt[b, j * pages + i], 0, 0))

    q3 = q.reshape(rows, 1, DA_W)
    kn3 = k_new.reshape(rows, 1, DA_W)
    vn3 = v_new.reshape(rows, 1, DA_HEADS * DA_VDIM)
    grid_spec = pltpu.PrefetchScalarGridSpec(
        num_scalar_prefetch=1,
        grid=(rows, n_pages // pages),
        in_specs=[per_b(q3), per_b(kn3), per_b(vn3),
                  pl.BlockSpec(lam_p.shape, lambda b, j, pt: (0, 0)),
                  pl.BlockSpec((1, DA_VDIM), lambda b, j, pt: (0, 0))]
        + [page_spec(i) for i in range(pages)] + [page_spec(i) for i in range(pages)],
        out_specs=pl.BlockSpec((1, DA_HEADS, DA_VDIM), lambda b, j, pt: (b, 0, 0)),
        scratch_shapes=[pltpu.VMEM((DA_SUB, DA_W), BF16), pltpu.VMEM((DA_SUB, 1), F32),
                        pltpu.VMEM((DA_SUB, 1), F32), pltpu.VMEM((DA_SUB, DA_HEADS * DA_VDIM), F32)],
    )
    out = pl.pallas_call(
        functools.partial(_da_decode_kernel, pages=pages, lam_init=lam_init),
        out_shape=jax.ShapeDtypeStruct((rows, DA_HEADS, DA_VDIM), BF16),
        grid_spec=grid_spec,
        compiler_params=_params(("parallel", "arbitrary")),
        name="da_attn_decode",
    )(page_table, q3, kn3, vn3, lam_p.astype(F32), subln.reshape(1, DA_VDIM).astype(F32),
      *([ck] * pages), *([cv] * pages))
    return out.reshape(rows, DA_HEADS * DA_VDIM)


def kernel(x_prompt, x_sample, state_C, state_n, state_m, cache_k, cache_v, page_table,
           ffn1_norm, ffn1_w_gu, ffn1_w_down, mix_norm, ffn2_norm, ffn2_w_gu, ffn2_w_down,
           ml_w_in, ml_gate_bias, ml_out_norm, ml_w_out,
           da_w_qkv, da_q_norm, da_k_norm, da_lambda, da_subln, da_w_out):
    depth = ffn1_norm.shape[0]
    bp, lp, d = x_prompt.shape
    bs, ls, _ = x_sample.shape
    assert ls == 1
    past_len = page_table.shape[1] * PAGE_SIZE
    pos_p = jnp.arange(lp, dtype=jnp.int32)
    pos_s = past_len + jnp.arange(ls, dtype=jnp.int32)

    xp = x_prompt.reshape(bp * lp, d)
    xs = x_sample.reshape(bs * ls, d)
    ml_p, ml_s, kv_p, kv_s = [], [], [], []
    pre_p = pre_s = None
    for i in range(depth):
        j = i // N_MIXERS
        w1 = _ffn_weights(ffn1_w_gu[i], ffn1_w_down[i])
        w2 = _ffn_weights(ffn2_w_gu[i], ffn2_w_down[i])
        xp = _ffn_call(xp, ffn1_norm[i], *w1, pre=pre_p)
        xs = _ffn_call(xs, ffn1_norm[i], *w1, pre=pre_s)
        if i % N_MIXERS == 0:
            q, kT, v, o, gates = _ml_proj_prompt_call(xp, mix_norm[i], ml_w_in[j], ml_gate_bias[j],
                                                      bp, lp)
            act_p, c_p, n_p, m_p = _ml_recur_call(q, kT, v, o, gates, ml_out_norm[j], bp, lp)
            ml_p.append((c_p, n_p, m_p))
            z, zT, gts = _ml_proj_decode_call(xs, mix_norm[i], ml_w_in[j], ml_gate_bias[j])
            act_s, c_s, n_s, m_s = _ml_decode_call(z, zT, gts, state_C[j], state_n[j], state_m[j],
                                                   ml_out_norm[j])
            ml_s.append((c_s, n_s, m_s))
            w_out = ml_w_out[j].astype(BF16)
        else:
            lam_init = _lambda_init(i)
            qp, kp, kpb, vp, vpb = _da_proj_call(xp, mix_norm[i], da_w_qkv[j], da_q_norm[j],
                                                 da_k_norm[j], pos_p, lp)
            act_p = _da_attn_call(qp, kpb, vpb, da_lambda[j], da_subln[j], bp, lp, lam_init)
            kv_p.append((kp.reshape(bp, lp, DA_SUB, DA_HEAD_DIM),
                         vp.reshape(bp, lp, DA_HEADS, DA_VDIM)))
            qs, ks, _, vs, _ = _da_proj_call(xs, mix_norm[i], da_w_qkv[j], da_q_norm[j],
                                             da_k_norm[j], jnp.repeat(pos_s, bs), bs)
            act_s = _da_decode_call(qs, ks, vs, cache_k[j], cache_v[j], page_table,
                                    da_lambda[j], da_subln[j], lam_init)
            kv_s.append((ks.reshape(bs, ls, DA_SUB, DA_HEAD_DIM),
                         vs.reshape(bs, ls, DA_HEADS, DA_VDIM)))
            w_out = da_w_out[j].astype(BF16)
        xp = _ffn_call(xp, ffn2_norm[i], *w2, pre=(act_p, w_out))
        xs = _ffn_call(xs, ffn2_norm[i], *w2, pre=(act_s, w_out))

    stack = lambda items, k: jnp.stack([it[k] for it in items])
    return (xp.reshape(bp, lp, d), xs.reshape(bs, ls, d),
            stack(ml_p, 0), stack(ml_p, 1), stack(ml_p, 2),
            stack(ml_s, 0), stack(ml_s, 1), stack(ml_s, 2),
            stack(kv_p, 0), stack(kv_p, 1), stack(kv_s, 0), stack(kv_s, 1))
```

```python
import functools
import math

import jax
import jax.numpy as jnp
from jax import lax
from jax.experimental import pallas as pl
from jax.experimental.pallas import tpu as pltpu

F32 = jnp.float32
BF16 = jnp.bfloat16

EPS = 1e-6
ML_HEADS = 4
ML_DK = 128
ML_DV = 256
ML_QK_W = ML_HEADS * ML_DK
ML_V_W = ML_HEADS * ML_DV
GATE_SOFTCAP = 15.0
DA_HEAD_DIM = 64
DA_HEADS = 8
DA_SUB = 2 * DA_HEADS
DA_VDIM = 128
DA_W = DA_SUB * DA_HEAD_DIM
ROPE_THETA = 500000.0
ROT_DIM = DA_HEAD_DIM // 4
ROT_HALF = ROT_DIM // 2
PAGE_SIZE = 128
N_MIXERS = 2
SCORE_SCALE_LOG2 = (DA_HEAD_DIM ** -0.5) * math.log2(math.e)

V7X_LANES = 128
V7X_MXU_DIM = 256
V7X_VMEM_LIMIT_BYTES = 56 * 1024 * 1024

FFN_CHUNK = 256
ML_CHUNK = 128
ATTN_Q_BLOCK = 256
ATTN_KV_BLOCK = 256
ATTN_HEADS_AHEAD = 2
DECODE_PAGES_PER_STEP = 8


def _lambda_init(layer):
    return 0.8 - 0.6 * math.exp(-0.3 * layer)


def _const_spec(shape):
    zeros = (0,) * len(shape)
    return pl.BlockSpec(shape, lambda *_: zeros, pipeline_mode=pl.Buffered(1))


def _params(semantics):
    return pltpu.CompilerParams(dimension_semantics=semantics,
                                vmem_limit_bytes=V7X_VMEM_LIMIT_BYTES)


def _row_tile(rows, target):
    t = min(rows, target)
    while rows % t:
        t //= 2
    return t


def _rms_rows(x, gain):
    ms = jnp.mean(x * x, axis=-1, keepdims=True)
    return x * lax.rsqrt(ms + EPS) * gain


def _ffn_kernel(*refs, has_pre, n_chunks):
    if has_pre:
        x_ref, a_ref, wo_ref, g_ref, wg_ref, wu_ref, wd_ref, o_ref, h_scr = refs
        x = x_ref[...] + jnp.dot(a_ref[...], wo_ref[...], preferred_element_type=F32)
    else:
        x_ref, g_ref, wg_ref, wu_ref, wd_ref, o_ref, h_scr = refs
        x = x_ref[...]
    h_scr[...] = _rms_rows(x, g_ref[...]).astype(BF16)
    o_ref[...] = x

    for c in range(n_chunks):
        h = h_scr[...]
        g = jnp.dot(h, wg_ref[c], preferred_element_type=F32)
        u = jnp.dot(h, wu_ref[c], preferred_element_type=F32)
        a = (0.5 * g * jax.nn.sigmoid(g) * u).astype(BF16)
        o_ref[...] += jnp.dot(a, wd_ref[c], preferred_element_type=F32)


def _ffn_call(x, gain, wg, wu, wd, pre=None, *, row_target=512):
    rows, d = x.shape
    n_chunks, _, tf = wg.shape
    tm = _row_tile(rows, row_target)
    row_spec = pl.BlockSpec((tm, d), lambda i: (i, 0))
    in_specs, args = [row_spec], [x]
    if pre is not None:
        a, wo = pre
        in_specs += [pl.BlockSpec((tm, a.shape[1]), lambda i: (i, 0)), _const_spec(wo.shape)]
        args += [a, wo]
    in_specs += [_const_spec((1, d)), _const_spec(wg.shape), _const_spec(wu.shape),
                 _const_spec(wd.shape)]
    args += [gain.reshape(1, d), wg, wu, wd]
    return pl.pallas_call(
        functools.partial(_ffn_kernel, has_pre=pre is not None, n_chunks=n_chunks),
        out_shape=jax.ShapeDtypeStruct((rows, d), F32),
        grid=(rows // tm,),
        in_specs=in_specs,
        out_specs=row_spec,
        scratch_shapes=[pltpu.VMEM((tm, d), BF16)],
        compiler_params=_params(("parallel",)),
        name="ffn_pre" if pre is not None else "ffn",
    )(*args)


def _ffn_weights(w_gu, w_down):
    d, two_ff = w_gu.shape
    d_ff = two_ff // 2
    tf = FFN_CHUNK
    while d_ff % tf:
        tf //= 2
    nc = d_ff // tf
    wg = w_gu[:, :d_ff].reshape(d, nc, tf).transpose(1, 0, 2).astype(BF16)
    wu = w_gu[:, d_ff:].reshape(d, nc, tf).transpose(1, 0, 2).astype(BF16)
    wd = w_down.reshape(nc, tf, d).astype(BF16)
    return wg, wu, wd


def _activate_gates(g, is_forget):
    g = GATE_SOFTCAP * jnp.tanh(g / GATE_SOFTCAP)
    log_sig = jnp.minimum(g, 0.0) - jnp.log1p(jnp.exp(-jnp.abs(g)))
    return jnp.where(is_forget, log_sig, g)


def _ml_proj_prompt_kernel(x_ref, g_ref, wqvo_ref, wkT_ref, wgT_ref, gb_ref,
                           q_ref, kT_ref, v_ref, o_ref, gates_ref):
    h = _rms_rows(x_ref[...], g_ref[...]).astype(BF16)
    z = jnp.dot(h, wqvo_ref[...], preferred_element_type=F32)
    q_ref[...] = z[:, :ML_QK_W].astype(BF16)
    v_ref[...] = z[:, ML_QK_W:ML_QK_W + ML_V_W].astype(BF16)
    o_ref[...] = z[:, ML_QK_W + ML_V_W:]
    nt = (((1,), (1,)), ((), ()))
    kT = lax.dot_general(wkT_ref[...], h, nt, preferred_element_type=F32)
    kT_ref[0] = (kT * (ML_DK ** -0.5)).astype(BF16)
    gT = lax.dot_general(wgT_ref[...], h, nt, preferred_element_type=F32) + gb_ref[...]
    row = lax.broadcasted_iota(jnp.int32, gT.shape, 0)
    gates_ref[0] = _activate_gates(gT, row >= ML_HEADS)


def _ml_proj_prompt_call(x, gain, w_in, gate_bias, batch, seq, *, row_target=512):
    rows, d = x.shape
    tm = _row_tile(seq, row_target)
    per_seq = seq // tm
    ng = 2 * ML_HEADS
    wqvo = jnp.concatenate([w_in[:, :ML_QK_W], w_in[:, 2 * ML_QK_W:2 * ML_QK_W + 2 * ML_V_W]],
                           axis=1).astype(BF16)
    wkT = w_in[:, ML_QK_W:2 * ML_QK_W].T.astype(BF16)
    wgT = w_in[:, 2 * ML_QK_W + 2 * ML_V_W:].T.astype(BF16)
    row = lambda w: pl.BlockSpec((tm, w), lambda i: (i, 0))
    by_seq = lambda r: pl.BlockSpec((1, r, tm), lambda i: (i // per_seq, 0, i % per_seq))
    return pl.pallas_call(
        _ml_proj_prompt_kernel,
        out_shape=(jax.ShapeDtypeStruct((rows, ML_QK_W), BF16),
                   jax.ShapeDtypeStruct((batch, ML_QK_W, seq), BF16),
                   jax.ShapeDtypeStruct((rows, ML_V_W), BF16),
                   jax.ShapeDtypeStruct((rows, ML_V_W), F32),
                   jax.ShapeDtypeStruct((batch, ng, seq), F32)),
        grid=(rows // tm,),
        in_specs=[row(d), _const_spec((1, d)), _const_spec(wqvo.shape), _const_spec(wkT.shape),
                  _const_spec(wgT.shape), _const_spec((ng, 1))],
        out_specs=(row(ML_QK_W), by_seq(ML_QK_W), row(ML_V_W), row(ML_V_W), by_seq(ng)),
        compiler_params=_params(("parallel",)),
        name="ml_proj_prompt",
    )(x, gain.reshape(1, d), wqvo, wkT, wgT, gate_bias.reshape(ng, 1).astype(F32))


def _ml_recur_kernel(q_ref, kT_ref, v_ref, o_ref, gates_ref, gain_ref,
                     act_ref, c_out, n_out, m_out, c_s, n_s, m_s, *, chunk):
    j = pl.program_id(1)

    @pl.when(j == 0)
    def _():
        c_s[...] = jnp.zeros_like(c_s)
        n_s[...] = jnp.zeros_like(n_s)
        m_s[...] = jnp.zeros_like(m_s)

    gates = gates_ref[0]
    lane = lax.broadcasted_iota(jnp.int32, gates.shape, 1)
    cum = gates
    step = 1
    while step < chunk:
        cum = cum + jnp.where(lane >= step, pltpu.roll(cum, step, axis=1), 0.0)
        step *= 2
    t_idx = lax.broadcasted_iota(jnp.int32, (chunk, chunk), 0)
    s_idx = lax.broadcasted_iota(jnp.int32, (chunk, chunk), 1)
    causal = s_idx <= t_idx
    nt = (((1,), (1,)), ((), ()))

    heads = range(ML_HEADS)
    dot = functools.partial(jnp.dot, preferred_element_type=F32)
    qh = [q_ref[0, :, h * ML_DK:(h + 1) * ML_DK] for h in heads]
    kTh = [kT_ref[0, h * ML_DK:(h + 1) * ML_DK, :] for h in heads]
    vh = [v_ref[0, :, h * ML_DV:(h + 1) * ML_DV] for h in heads]
    c_old = [c_s[h] for h in heads]
    n_row = [n_s[h:h + 1, :] for h in heads]
    m_prev = [m_s[h:h + 1, 0:1] for h in heads]
    qk = [dot(qh[h], kTh[h]) for h in heads]
    qc = [dot(qh[h], c_old[h].astype(BF16)) for h in heads]
    qn = [jnp.sum(qh[h].astype(F32) * n_row[h], axis=1, keepdims=True) for h in heads]

    li_r = [gates[h:h + 1, :] for h in heads]
    lf_r = [gates[ML_HEADS + h:ML_HEADS + h + 1, :] for h in heads]
    b_r = [cum[ML_HEADS + h:ML_HEADS + h + 1, :] for h in heads]
    b_last = [b[:, chunk - 1:chunk] for b in b_r]
    b_c = [jnp.sum(jnp.where(causal, lf, 0.0), axis=1, keepdims=True) for lf in lf_r]
    d_mat = [jnp.where(causal, b_c[h] - b_r[h] + li_r[h], -jnp.inf) for h in heads]
    a = [b_c[h] + m_prev[h] for h in heads]
    mt = [jnp.maximum(a[h], jnp.max(d_mat[h], axis=1, keepdims=True)) for h in heads]
    s = [qk[h] * jnp.exp(d_mat[h] - mt[h]) for h in heads]
    sv = [dot(s[h].astype(BF16), vh[h]) for h in heads]

    g_r = [b_last[h] - b_r[h] + li_r[h] for h in heads]
    m_new = [jnp.maximum(b_last[h] + m_prev[h], jnp.max(g_r[h], axis=1, keepdims=True))
             for h in heads]
    decay = [jnp.exp(b_last[h] + m_prev[h] - m_new[h]) for h in heads]
    wk = [jnp.exp(g_r[h] - m_new[h]) for h in heads]
    kv = [dot((kTh[h].astype(F32) * wk[h]).astype(BF16), vh[h]) for h in heads]
    nk = [lax.dot_general(wk[h].astype(BF16), kTh[h], nt, preferred_element_type=F32)
          for h in heads]

    for h in heads:
        w_inter = jnp.exp(a[h] - mt[h])
        num = w_inter * qc[h] + sv[h]
        den = w_inter * qn[h] + jnp.sum(s[h], axis=1, keepdims=True)
        hid = num / jnp.maximum(jnp.abs(den), jnp.exp(-mt[h]))
        hn = _rms_rows(hid, gain_ref[h:h + 1, :])
        og = jax.nn.sigmoid(o_ref[0, :, h * ML_DV:(h + 1) * ML_DV])
        act_ref[0, :, h * ML_DV:(h + 1) * ML_DV] = (hn * og).astype(BF16)
    for h in heads:
        c_s[h] = decay[h] * c_old[h] + kv[h]
        n_s[h:h + 1, :] = decay[h] * n_row[h] + nk[h]
        m_s[h:h + 1, :] = jnp.broadcast_to(m_new[h], (1, V7X_LANES))

    @pl.when(j == pl.num_programs(1) - 1)
    def _():
        c_out[0] = c_s[...]
        n_out[0] = n_s[0:ML_HEADS, :]
        lane_m = lax.broadcasted_iota(jnp.int32, (1, V7X_LANES), 1)
        m_row = jnp.zeros((1, V7X_LANES), F32)
        for h in range(ML_HEADS):
            m_row = jnp.where(lane_m == h, m_s[h:h + 1, :], m_row)
        m_out[0] = m_row


def _ml_recur_call(q, kT, v, o, gates, out_gain, batch, seq):
    chunk = _row_tile(seq, ML_CHUNK)
    n_steps = seq // chunk
    ng = 2 * ML_HEADS
    q3 = q.reshape(batch, seq, ML_QK_W)
    v3 = v.reshape(batch, seq, ML_V_W)
    o3 = o.reshape(batch, seq, ML_V_W)
    tok = lambda w: pl.BlockSpec((1, chunk, w), lambda b, j: (b, j, 0))
    col = lambda r: pl.BlockSpec((1, r, chunk), lambda b, j: (b, 0, j))
    per_b = lambda *s: pl.BlockSpec((1,) + s, lambda b, j: (b,) + (0,) * len(s))
    act, c_new, n_new, m_new = pl.pallas_call(
        functools.partial(_ml_recur_kernel, chunk=chunk),
        out_shape=(jax.ShapeDtypeStruct((batch, seq, ML_V_W), BF16),
                   jax.ShapeDtypeStruct((batch, ML_HEADS, ML_DK, ML_DV), F32),
                   jax.ShapeDtypeStruct((batch, ML_HEADS, ML_DK), F32),
                   jax.ShapeDtypeStruct((batch, 1, V7X_LANES), F32)),
        grid=(batch, n_steps),
        in_specs=[tok(ML_QK_W), col(ML_QK_W), tok(ML_V_W), tok(ML_V_W), col(ng),
                  pl.BlockSpec((ML_HEADS, ML_DV), lambda b, j: (0, 0))],
        out_specs=(tok(ML_V_W), per_b(ML_HEADS, ML_DK, ML_DV), per_b(ML_HEADS, ML_DK),
                   per_b(1, V7X_LANES)),
        scratch_shapes=[pltpu.VMEM((ML_HEADS, ML_DK, ML_DV), F32),
                        pltpu.VMEM((8, ML_DK), F32),
                        pltpu.VMEM((8, V7X_LANES), F32)],
        compiler_params=_params(("parallel", "arbitrary")),
        name="ml_recur",
    )(q3, kT, v3, o3, gates, out_gain.astype(F32))
    return act.reshape(batch * seq, ML_V_W), c_new, n_new, m_new[:, 0, :ML_HEADS]


def _ml_proj_decode_kernel(x_ref, g_ref, w_ref, wqkT_ref, wg_ref, gb_ref, z_ref, zT_ref, gates_ref):
    h = _rms_rows(x_ref[...], g_ref[...]).astype(BF16)
    z = jnp.dot(h, w_ref[...], preferred_element_type=F32)
    col = lax.broadcasted_iota(jnp.int32, z.shape, 1)
    is_k = (col >= ML_QK_W) & (col < 2 * ML_QK_W)
    z_ref[...] = jnp.where(is_k, z * (ML_DK ** -0.5), z)
    nt = (((1,), (1,)), ((), ()))
    zT = lax.dot_general(wqkT_ref[...], h, nt, preferred_element_type=F32)
    rowi = lax.broadcasted_iota(jnp.int32, zT.shape, 0)
    zT_ref[...] = jnp.where(rowi >= ML_QK_W, zT * (ML_DK ** -0.5), zT)
    g = jnp.dot(h, wg_ref[...], preferred_element_type=F32) + gb_ref[...]
    lane = lax.broadcasted_iota(jnp.int32, g.shape, 1)
    gates_ref[...] = _activate_gates(g, lane >= ML_HEADS)


def _ml_proj_decode_call(x, gain, w_in, gate_bias):
    rows, d = x.shape
    ng = 2 * ML_HEADS
    w_main = w_in[:, :2 * ML_QK_W + 2 * ML_V_W].astype(BF16)
    wqkT = w_in[:, :2 * ML_QK_W].T.astype(BF16)
    wg = jnp.pad(w_in[:, 2 * ML_QK_W + 2 * ML_V_W:], ((0, 0), (0, V7X_LANES - ng))).astype(BF16)
    gb = jnp.pad(gate_bias.astype(F32), (0, V7X_LANES - ng)).reshape(1, V7X_LANES)
    full = lambda a: _const_spec(a.shape)
    args = (x, gain.reshape(1, d), w_main, wqkT, wg, gb)
    out_shape = (jax.ShapeDtypeStruct((rows, w_main.shape[1]), F32),
                 jax.ShapeDtypeStruct((2 * ML_QK_W, rows), F32),
                 jax.ShapeDtypeStruct((rows, V7X_LANES), F32))
    return pl.pallas_call(
        _ml_proj_decode_kernel,
        out_shape=out_shape,
        grid=(1,),
        in_specs=[full(a) for a in args],
        out_specs=tuple(pl.BlockSpec(s.shape, lambda i: (0, 0)) for s in out_shape),
        compiler_params=_params(("arbitrary",)),
        name="ml_proj_decode",
    )(*args)


def _ml_decode_kernel(z_ref, zT_ref, gates_ref, c_ref, n_ref, m_ref, gain_ref,
                      act_ref, c_out, n_out, m_out):
    b = pl.program_id(0)
    z = z_ref[pl.ds(b, 1), :]
    g = gates_ref[pl.ds(b, 1), :]
    m_in = m_ref[pl.ds(b, 1), :]
    zT = zT_ref[...]
    pick = lax.broadcasted_iota(jnp.int32, zT.shape, 1) == b
    z_col = jnp.sum(jnp.where(pick, zT, 0.0), axis=1, keepdims=True)
    lane_m = lax.broadcasted_iota(jnp.int32, (1, V7X_LANES), 1)
    m_row = jnp.zeros((1, V7X_LANES), F32)

    for h in range(ML_HEADS):
        li = g[:, h:h + 1]
        lf = g[:, ML_HEADS + h:ML_HEADS + h + 1]
        m_prev = m_in[:, h:h + 1]
        q_row = z[:, h * ML_DK:(h + 1) * ML_DK]
        k_row = z[:, ML_QK_W + h * ML_DK:ML_QK_W + (h + 1) * ML_DK]
        v_row = z[:, 2 * ML_QK_W + h * ML_DV:2 * ML_QK_W + (h + 1) * ML_DV]
        o_row = z[:, 2 * ML_QK_W + ML_V_W + h * ML_DV:2 * ML_QK_W + ML_V_W + (h + 1) * ML_DV]
        q_col = z_col[h * ML_DK:(h + 1) * ML_DK, :]
        k_col = z_col[ML_QK_W + h * ML_DK:ML_QK_W + (h + 1) * ML_DK, :]
        c_old = c_ref[0, h]
        n_row = n_ref[0, h:h + 1, :]

        a = lf + m_prev
        mt = jnp.maximum(a, li)
        w_inter = jnp.exp(a - mt)
        s = jnp.sum(q_row * k_row, axis=1, keepdims=True) * jnp.exp(li - mt)
        num = w_inter * jnp.sum(q_col * c_old, axis=0, keepdims=True) + s * v_row
        den = w_inter * jnp.sum(q_row * n_row, axis=1, keepdims=True) + s
        hid = num / jnp.maximum(jnp.abs(den), jnp.exp(-mt))
        hn = _rms_rows(hid, gain_ref[h:h + 1, :])
        act_ref[0, :, h * ML_DV:(h + 1) * ML_DV] = (hn * jax.nn.sigmoid(o_row)).astype(BF16)

        m_new = jnp.maximum(a, li)
        decay = jnp.exp(a - m_new)
        wk = jnp.exp(li - m_new)
        c_out[0, h] = decay * c_old + (wk * k_col) * v_row
        n_out[0, h:h + 1, :] = decay * n_row + wk * k_row
        m_row = jnp.where(lane_m == h, m_new, m_row)
    m_out[0] = m_row


def _ml_decode_call(z, zT, gates, state_c, state_n, state_m, out_gain):
    rows = z.shape[0]
    full = lambda a: _const_spec(a.shape)
    per_b = lambda *s: pl.BlockSpec((1,) + s, lambda b: (b,) + (0,) * len(s))
    act, c_new, n_new, m_new = pl.pallas_call(
        _ml_decode_kernel,
        out_shape=(jax.ShapeDtypeStruct((rows, 1, ML_V_W), BF16),
                   jax.ShapeDtypeStruct(state_c.shape, F32),
                   jax.ShapeDtypeStruct(state_n.shape, F32),
                   jax.ShapeDtypeStruct((rows, 1, V7X_LANES), F32)),
        grid=(rows,),
        in_specs=[full(z), full(zT), full(gates), per_b(ML_HEADS, ML_DK, ML_DV),
                  per_b(ML_HEADS, ML_DK), full(state_m), full(out_gain)],
        out_specs=(per_b(1, ML_V_W), per_b(ML_HEADS, ML_DK, ML_DV), per_b(ML_HEADS, ML_DK),
                   per_b(1, V7X_LANES)),
        compiler_params=_params(("arbitrary",)),
        name="ml_decode",
    )(z, zT, gates, state_c, state_n, state_m, out_gain.astype(F32))
    return act.reshape(rows, ML_V_W), c_new, n_new, m_new[:, 0, :ML_HEADS]


def _subhead_mean_square(t, bsum):
    t2 = t * t
    hi = t2.astype(BF16)
    lo = (t2 - hi.astype(F32)).astype(BF16)
    parts = []
    for j in range(t.shape[1] // V7X_MXU_DIM):
        sl = slice(j * V7X_MXU_DIM, (j + 1) * V7X_MXU_DIM)
        parts.append(jnp.dot(hi[:, sl], bsum, preferred_element_type=F32)
                     + jnp.dot(lo[:, sl], bsum, preferred_element_type=F32))
    return jnp.concatenate(parts, axis=1)


def _rope(t, cos, sin_signed):
    lane = lax.broadcasted_iota(jnp.int32, (t.shape[0], V7X_LANES), 1)
    first = (lane % DA_HEAD_DIM) < ROT_HALF
    parts = []
    for j in range(t.shape[1] // V7X_LANES):
        tj = t[:, j * V7X_LANES:(j + 1) * V7X_LANES]
        partner = jnp.where(first, pltpu.roll(tj, V7X_LANES - ROT_HALF, axis=1),
                            pltpu.roll(tj, ROT_HALF, axis=1))
        parts.append(tj * cos + partner * sin_signed)
    return jnp.concatenate(parts, axis=1)


def _da_proj_kernel(*refs, for_prompt):
    if for_prompt:
        (x_ref, g_ref, w_ref, qg_ref, kg_ref, bsum_ref, cos_ref, sin_ref, wvT_ref,
         q_ref, k_ref, v_ref, kb_ref, vT_ref) = refs
    else:
        (x_ref, g_ref, w_ref, qg_ref, kg_ref, bsum_ref, cos_ref, sin_ref,
         q_ref, k_ref, v_ref) = refs
    h = _rms_rows(x_ref[...], g_ref[...]).astype(BF16)
    z = jnp.dot(h, w_ref[...], preferred_element_type=F32)
    bsum = bsum_ref[...]
    cos = cos_ref[...]
    sin = sin_ref[...]
    q = z[:, :DA_W]
    k = z[:, DA_W:2 * DA_W]
    q = _rope(q * lax.rsqrt(_subhead_mean_square(q, bsum) + EPS) * qg_ref[...], cos, sin)
    k = _rope(k * lax.rsqrt(_subhead_mean_square(k, bsum) + EPS) * kg_ref[...], cos, sin)
    q_ref[...] = (q * SCORE_SCALE_LOG2).astype(BF16)
    k_ref[...] = k
    v_ref[...] = z[:, 2 * DA_W:]
    if for_prompt:
        kb_ref[...] = k.astype(BF16)
        nt = (((1,), (1,)), ((), ()))
        vT_ref[0] = lax.dot_general(wvT_ref[...], h, nt, preferred_element_type=F32).astype(BF16)


def _rope_tables(pos):
    inv_freq = jnp.power(ROPE_THETA, -jnp.arange(0, ROT_DIM, 2, dtype=F32) / ROT_DIM)
    ang = pos.astype(F32)[:, None] * inv_freq[None, :]
    cos, sin = jnp.cos(ang), jnp.sin(ang)
    ones = jnp.ones((pos.shape[0], DA_HEAD_DIM - ROT_DIM), F32)
    cos64 = jnp.concatenate([cos, cos, ones], axis=1)
    sin64 = jnp.concatenate([-sin, sin, 0.0 * ones], axis=1)
    return jnp.tile(cos64, (1, 2)), jnp.tile(sin64, (1, 2))


def _da_proj_call(x, gain, w_qkv, q_gain, k_gain, pos, seq, *, for_prompt, row_target=512):
    rows, d = x.shape
    tm = _row_tile(seq, row_target)
    per_seq = seq // tm
    w = w_qkv.astype(BF16)
    cos, sin = _rope_tables(pos)
    blk = jnp.arange(V7X_MXU_DIM) // DA_HEAD_DIM
    bsum = jnp.where(blk[:, None] == blk[None, :], 1.0 / DA_HEAD_DIM, 0.0).astype(BF16)
    qg = jnp.tile(q_gain.astype(F32), DA_SUB).reshape(1, DA_W)
    kg = jnp.tile(k_gain.astype(F32), DA_SUB).reshape(1, DA_W)
    row = lambda wd: pl.BlockSpec((tm, wd), lambda i: (i, 0))
    table = pl.BlockSpec((tm, V7X_LANES), lambda i: (i % per_seq, 0))
    n_v = w.shape[1] - 2 * DA_W
    in_specs = [row(d), _const_spec((1, d)), _const_spec(w.shape), _const_spec((1, DA_W)),
                _const_spec((1, DA_W)), _const_spec(bsum.shape), table, table]
    args = [x, gain.reshape(1, d), w, qg, kg, bsum, cos, sin]
    out_shape = [jax.ShapeDtypeStruct((rows, DA_W), BF16), jax.ShapeDtypeStruct((rows, DA_W), F32),
                 jax.ShapeDtypeStruct((rows, n_v), F32)]
    out_specs = [row(DA_W), row(DA_W), row(n_v)]
    if for_prompt:
        wvT = w_qkv[:, 2 * DA_W:].T.astype(BF16)
        in_specs.append(_const_spec(wvT.shape))
        args.append(wvT)
        out_shape += [jax.ShapeDtypeStruct((rows, DA_W), BF16),
                      jax.ShapeDtypeStruct((rows // seq, n_v, seq), BF16)]
        out_specs += [row(DA_W),
                      pl.BlockSpec((1, n_v, tm), lambda i: (i // per_seq, 0, i % per_seq))]
    return pl.pallas_call(
        functools.partial(_da_proj_kernel, for_prompt=for_prompt),
        out_shape=tuple(out_shape),
        grid=(rows // tm,),
        in_specs=in_specs,
        out_specs=tuple(out_specs),
        compiler_params=_params(("parallel",)),
        name="da_proj_prompt" if for_prompt else "da_proj_decode",
    )(*args)


def _lambda_value(lam_ref, lam_init):
    lp = lam_ref[...]
    s1 = jnp.sum(lp[0:1, :] * lp[1:2, :], axis=1, keepdims=True)
    s2 = jnp.sum(lp[2:3, :] * lp[3:4, :], axis=1, keepdims=True)
    return jnp.exp(s1) - jnp.exp(s2) + lam_init


def _da_attn_kernel(q_ref, k_ref, vT_ref, lam_ref, subln_ref, o_ref, q2_s, m_s, l_s, acc_s,
                    *, tq, tk, lam_init):
    i = pl.program_id(1)
    nt = (((1,), (1,)), ((), ()))
    lane = lax.broadcasted_iota(jnp.int32, (tq, V7X_LANES), 1)
    for h in range(DA_HEADS):
        qp = q_ref[0, :, h * DA_VDIM:(h + 1) * DA_VDIM]
        zero = jnp.zeros_like(qp)
        q2_s[h, 0:tq, :] = jnp.where(lane < DA_HEAD_DIM, qp, zero)
        q2_s[h, tq:2 * tq, :] = jnp.where(lane >= DA_HEAD_DIM, qp, zero)
    m_s[...] = jnp.full(m_s.shape, -jnp.inf, F32)
    l_s[...] = jnp.zeros_like(l_s)
    acc_s[...] = jnp.zeros_like(acc_s)

    key_l = lax.broadcasted_iota(jnp.int32, (tk, 2 * tq), 0)
    qry_g = i * tq + lax.broadcasted_iota(jnp.int32, (tk, 2 * tq), 1) % tq

    def block(j, masked):
        start = pl.multiple_of(j * tk, tk)

        def scores(h):
            kb = k_ref[0, pl.ds(start, tk), h * DA_VDIM:(h + 1) * DA_VDIM]
            return lax.dot_general(kb, q2_s[h], nt, preferred_element_type=F32)

        ahead = [scores(h) for h in range(ATTN_HEADS_AHEAD)]
        for h in range(DA_HEADS):
            hs = slice(h * DA_VDIM, (h + 1) * DA_VDIM)
            s = ahead.pop(0)
            if h + ATTN_HEADS_AHEAD < DA_HEADS:
                ahead.append(scores(h + ATTN_HEADS_AHEAD))
            if masked:
                s = jnp.where(j * tk + key_l <= qry_g, s, -jnp.inf)
            m_old = m_s[h]
            m_new = jnp.maximum(m_old, jnp.max(s, axis=0, keepdims=True))
            alpha = jnp.exp2(m_old - m_new)
            p = jnp.exp2(s - m_new)
            l_s[h] = alpha * l_s[h] + jnp.sum(p, axis=0, keepdims=True)
            vTb = vT_ref[0, hs, pl.ds(start, tk)]
            acc_s[h] = alpha * acc_s[h] + jnp.dot(vTb, p.astype(BF16),
                                                  preferred_element_type=F32)
            m_s[h] = m_new

    n_full = (i * tq) // tk

    def full_block(j, carry):
        block(j, False)
        return carry

    lax.fori_loop(0, n_full, full_block, 0)
    block(n_full, True)

    lam = _lambda_value(lam_ref, lam_init)
    for h in range(DA_HEADS):
        oT = acc_s[h] / l_s[h]
        oT = oT[:, :tq] - lam * oT[:, tq:]
        ms = jnp.mean(oT * oT, axis=0, keepdims=True)
        oT = oT * lax.rsqrt(ms + EPS) * (subln_ref[...] * (1.0 - lam_init))
        o_ref[0, :, h * DA_VDIM:(h + 1) * DA_VDIM] = oT.T.astype(BF16)


def _da_attn_call(q, kb, vT, lam_p, subln, batch, seq, lam_init):
    tk = _row_tile(seq, ATTN_KV_BLOCK)
    tq = min(_row_tile(seq, ATTN_Q_BLOCK), tk)
    assert tk % tq == 0
    w = q.shape[1]
    q3, k3 = (t.reshape(batch, seq, t.shape[1]) for t in (q, kb))
    out = pl.pallas_call(
        functools.partial(_da_attn_kernel, tq=tq, tk=tk, lam_init=lam_init),
        out_shape=jax.ShapeDtypeStruct((batch, seq, vT.shape[1]), BF16),
        grid=(batch, seq // tq),
        in_specs=[pl.BlockSpec((1, tq, w), lambda b, i: (b, i, 0)),
                  pl.BlockSpec((1, seq, w), lambda b, i: (b, 0, 0)),
                  pl.BlockSpec((1, vT.shape[1], seq), lambda b, i: (b, 0, 0)),
                  pl.BlockSpec(lam_p.shape, lambda b, i: (0, 0)),
                  pl.BlockSpec((DA_VDIM, 1), lambda b, i: (0, 0))],
        out_specs=pl.BlockSpec((1, tq, vT.shape[1]), lambda b, i: (b, i, 0)),
        scratch_shapes=[pltpu.VMEM((DA_HEADS, 2 * tq, DA_VDIM), BF16),
                        pltpu.VMEM((DA_HEADS, 1, 2 * tq), F32),
                        pltpu.VMEM((DA_HEADS, 1, 2 * tq), F32),
                        pltpu.VMEM((DA_HEADS, DA_VDIM, 2 * tq), F32)],
        compiler_params=_params(("parallel", "arbitrary")),
        name="da_attn_prompt",
    )(q3, k3, vT, lam_p.astype(F32), subln.reshape(DA_VDIM, 1).astype(F32))
    return out.reshape(batch * seq, vT.shape[1])


def _da_decode_kernel(pt_ref, q_ref, kn_ref, vn_ref, lam_ref, subln_ref, *rest, pages, lam_init):
    k_refs = rest[:pages]
    v_refs = rest[pages:2 * pages]
    o_ref, qbd_s, m_s, l_s, acc_s = rest[2 * pages:]
    j = pl.program_id(1)
    nt = (((1,), (1,)), ((), ()))

    @pl.when(j == 0)
    def _():
        q = q_ref[0]
        sub = lax.broadcasted_iota(jnp.int32, (DA_SUB, DA_W), 0)
        col = lax.broadcasted_iota(jnp.int32, (DA_SUB, DA_W), 1)
        qb = jnp.broadcast_to(q.astype(F32), (DA_SUB, DA_W))
        qbd_s[...] = jnp.where(col // DA_HEAD_DIM == sub, qb, 0.0).astype(BF16)
        m_s[...] = jnp.full(m_s.shape, -jnp.inf, F32)
        l_s[...] = jnp.zeros_like(l_s)
        acc_s[...] = jnp.zeros_like(acc_s)

    qbd = qbd_s[...]
    s = jnp.concatenate(
        [lax.dot_general(qbd, k_refs[i][...].astype(BF16), nt, preferred_element_type=F32)
         for i in range(pages)], axis=1)
    m_old = m_s[...]
    m_new = jnp.maximum(m_old, jnp.max(s, axis=1, keepdims=True))
    alpha = jnp.exp2(m_old - m_new)
    p = jnp.exp2(s - m_new)
    l_s[...] = alpha * l_s[...] + jnp.sum(p, axis=1, keepdims=True)
    pv = jnp.zeros(acc_s.shape, F32)
    for i in range(pages):
        pv += jnp.dot(p[:, i * PAGE_SIZE:(i + 1) * PAGE_SIZE].astype(BF16),
                      v_refs[i][...].astype(BF16), preferred_element_type=F32)
    acc_s[...] = alpha * acc_s[...] + pv
    m_s[...] = m_new

    @pl.when(j == pl.num_programs(1) - 1)
    def _():
        lam = _lambda_value(lam_ref, lam_init)
        m_p = m_s[...]
        s_self = jnp.sum(qbd.astype(F32) * kn_ref[0], axis=1, keepdims=True)
        m_f = jnp.maximum(m_p, s_self)
        a_f = jnp.exp2(m_p - m_f)
        p_self = jnp.exp2(s_self - m_f)
        l_f = a_f * l_s[...] + p_self
        o = (a_f * acc_s[...] + p_self * vn_ref[0]) / l_f
        for h in range(DA_HEADS):
            hs = slice(h * DA_VDIM, (h + 1) * DA_VDIM)
            oh = o[2 * h:2 * h + 1, hs] - lam * o[2 * h + 1:2 * h + 2, hs]
            oh = _rms_rows(oh, subln_ref[...]) * (1.0 - lam_init)
            o_ref[0, h:h + 1, :] = oh.astype(BF16)


def _da_decode_call(q, k_new, v_new, cache_k, cache_v, layer, page_table, lam_p, subln, lam_init):
    rows = q.shape[0]
    n_pool = cache_k.shape[1]
    n_pages = page_table.shape[1]
    pages = DECODE_PAGES_PER_STEP
    while n_pages % pages:
        pages //= 2
    ck = cache_k.reshape(cache_k.shape[0] * n_pool, PAGE_SIZE, DA_W)
    cv = cache_v.reshape(cache_v.shape[0] * n_pool, PAGE_SIZE, DA_HEADS * DA_VDIM)
    base = layer * n_pool
    per_b = lambda a: pl.BlockSpec((1,) + a.shape[1:], lambda b, j, pt: (b, 0, 0))

    def page_spec(i):
        return pl.BlockSpec((None, PAGE_SIZE, DA_W),
                            lambda b, j, pt: (base + pt[b, j * pages + i], 0, 0))

    q3 = q.reshape(rows, 1, DA_W)
    kn3 = k_new.reshape(rows, 1, DA_W)
    vn3 = v_new.reshape(rows, 1, DA_HEADS * DA_VDIM)
    grid_spec = pltpu.PrefetchScalarGridSpec(
        num_scalar_prefetch=1,
        grid=(rows, n_pages // pages),
        in_specs=[per_b(q3), per_b(kn3), per_b(vn3),
                  pl.BlockSpec(lam_p.shape, lambda b, j, pt: (0, 0)),
                  pl.BlockSpec((1, DA_VDIM), lambda b, j, pt: (0, 0))]
        + [page_spec(i) for i in range(pages)] + [page_spec(i) for i in range(pages)],
        out_specs=pl.BlockSpec((1, DA_HEADS, DA_VDIM), lambda b, j, pt: (b, 0, 0)),
        scratch_shapes=[pltpu.VMEM((DA_SUB, DA_W), BF16), pltpu.VMEM((DA_SUB, 1), F32),
                        pltpu.VMEM((DA_SUB, 1), F32), pltpu.VMEM((DA_SUB, DA_HEADS * DA_VDIM), F32)],
    )
    out = pl.pallas_call(
        functools.partial(_da_decode_kernel, pages=pages, lam_init=lam_init),
        out_shape=jax.ShapeDtypeStruct((rows, DA_HEADS, DA_VDIM), BF16),
        grid_spec=grid_spec,
        compiler_params=_params(("parallel", "arbitrary")),
        name="da_attn_decode",
    )(page_table, q3, kn3, vn3, lam_p.astype(F32), subln.reshape(1, DA_VDIM).astype(F32),
      *([ck] * pages), *([cv] * pages))
    return out.reshape(rows, DA_HEADS * DA_VDIM)


def kernel(x_prompt, x_sample, state_C, state_n, state_m, cache_k, cache_v, page_table,
           ffn1_norm, ffn1_w_gu, ffn1_w_down, mix_norm, ffn2_norm, ffn2_w_gu, ffn2_w_down,
           ml_w_in, ml_gate_bias, ml_out_norm, ml_w_out,
           da_w_qkv, da_q_norm, da_k_norm, da_lambda, da_subln, da_w_out):
    depth = ffn1_norm.shape[0]
    bp, lp, d = x_prompt.shape
    bs, ls, _ = x_sample.shape
    assert ls == 1
    past_len = page_table.shape[1] * PAGE_SIZE
    pos_p = jnp.arange(lp, dtype=jnp.int32)
    pos_s = past_len + jnp.arange(ls, dtype=jnp.int32)

    xp = x_prompt.reshape(bp * lp, d)
    xs = x_sample.reshape(bs * ls, d)
    ml_p, ml_s, kv_p, kv_s = [], [], [], []
    for i in range(depth):
        j = i // N_MIXERS
        w1 = _ffn_weights(ffn1_w_gu[i], ffn1_w_down[i])
        w2 = _ffn_weights(ffn2_w_gu[i], ffn2_w_down[i])
        xp = _ffn_call(xp, ffn1_norm[i], *w1)
        xs = _ffn_call(xs, ffn1_norm[i], *w1)
        if i % N_MIXERS == 0:
            q, kT, v, o, gates = _ml_proj_prompt_call(xp, mix_norm[i], ml_w_in[j], ml_gate_bias[j],
                                                      bp, lp)
            act_p, c_p, n_p, m_p = _ml_recur_call(q, kT, v, o, gates, ml_out_norm[j], bp, lp)
            ml_p.append((c_p, n_p, m_p))
            z, zT, gts = _ml_proj_decode_call(xs, mix_norm[i], ml_w_in[j], ml_gate_bias[j])
            act_s, c_s, n_s, m_s = _ml_decode_call(z, zT, gts, state_C[j], state_n[j], state_m[j],
                                                   ml_out_norm[j])
            ml_s.append((c_s, n_s, m_s))
            w_out = ml_w_out[j].astype(BF16)
        else:
            lam_init = _lambda_init(i)
            qp, kp, vp, kpb, vpT = _da_proj_call(xp, mix_norm[i], da_w_qkv[j], da_q_norm[j],
                                                 da_k_norm[j], pos_p, lp, for_prompt=True)
            act_p = _da_attn_call(qp, kpb, vpT, da_lambda[j], da_subln[j], bp, lp, lam_init)
            kv_p.append((kp.reshape(bp, lp, DA_SUB, DA_HEAD_DIM),
                         vp.reshape(bp, lp, DA_HEADS, DA_VDIM)))
            qs, ks, vs = _da_proj_call(xs, mix_norm[i], da_w_qkv[j], da_q_norm[j], da_k_norm[j],
                                       jnp.repeat(pos_s, bs), bs, for_prompt=False)
            act_s = _da_decode_call(qs, ks, vs, cache_k, cache_v, j, page_table,
                                    da_lambda[j], da_subln[j], lam_init)
            kv_s.append((ks.reshape(bs, ls, DA_SUB, DA_HEAD_DIM),
                         vs.reshape(bs, ls, DA_HEADS, DA_VDIM)))
            w_out = da_w_out[j].astype(BF16)
        xp = _ffn_call(xp, ffn2_norm[i], *w2, pre=(act_p, w_out))
        xs = _ffn_call(xs, ffn2_norm[i], *w2, pre=(act_s, w_out))

    stack = lambda items, k: jnp.stack([it[k] for it in items])
    return (xp.reshape(bp, lp, d), xs.reshape(bs, ls, d),
            stack(ml_p, 0), stack(ml_p, 1), stack(ml_p, 2),
            stack(ml_s, 0), stack(ml_s, 1), stack(ml_s, 2),
            stack(kv_p, 0), stack(kv_p, 1), stack(kv_s, 0), stack(kv_s, 1))
```

```python
import functools
import math

import jax
import jax.numpy as jnp
from jax import lax
from jax.experimental import pallas as pl
from jax.experimental.pallas import tpu as pltpu

F32 = jnp.float32
BF16 = jnp.bfloat16

EPS = 1e-6
ML_HEADS = 4
ML_DK = 128
ML_DV = 256
ML_QK_W = ML_HEADS * ML_DK
ML_V_W = ML_HEADS * ML_DV
GATE_SOFTCAP = 15.0
DA_HEAD_DIM = 64
DA_HEADS = 8
DA_SUB = 2 * DA_HEADS
DA_VDIM = 128
DA_W = DA_SUB * DA_HEAD_DIM
ROPE_THETA = 500000.0
ROT_DIM = DA_HEAD_DIM // 4
ROT_HALF = ROT_DIM // 2
PAGE_SIZE = 128
N_MIXERS = 2
SCORE_SCALE_LOG2 = (DA_HEAD_DIM ** -0.5) * math.log2(math.e)

V7X_LANES = 128
V7X_MXU_DIM = 256
V7X_VMEM_LIMIT_BYTES = 56 * 1024 * 1024

FFN_CHUNK = 256
ML_CHUNK = 128
ATTN_Q_BLOCK = 256
ATTN_KV_BLOCK = 256
ATTN_HEADS_AHEAD = 2
DECODE_PAGES_PER_STEP = 8


def _lambda_init(layer):
    return 0.8 - 0.6 * math.exp(-0.3 * layer)


def _const_spec(shape):
    zeros = (0,) * len(shape)
    return pl.BlockSpec(shape, lambda *_: zeros, pipeline_mode=pl.Buffered(1))


def _params(semantics):
    return pltpu.CompilerParams(dimension_semantics=semantics,
                                vmem_limit_bytes=V7X_VMEM_LIMIT_BYTES)


def _row_tile(rows, target):
    t = min(rows, target)
    while rows % t:
        t //= 2
    return t


def _rms_rows(x, gain):
    ms = jnp.mean(x * x, axis=-1, keepdims=True)
    return x * lax.rsqrt(ms + EPS) * gain


def _ffn_kernel(*refs, has_pre, n_chunks):
    if has_pre:
        x_ref, a_ref, wo_ref, g_ref, wg_ref, wu_ref, wd_ref, o_ref, h_scr = refs
        x = x_ref[...] + jnp.dot(a_ref[...], wo_ref[...], preferred_element_type=F32)
    else:
        x_ref, g_ref, wg_ref, wu_ref, wd_ref, o_ref, h_scr = refs
        x = x_ref[...]
    h_scr[...] = _rms_rows(x, g_ref[...]).astype(BF16)
    o_ref[...] = x

    for c in range(n_chunks):
        h = h_scr[...]
        g = jnp.dot(h, wg_ref[c], preferred_element_type=F32)
        u = jnp.dot(h, wu_ref[c], preferred_element_type=F32)
        a = (0.5 * g * jax.nn.sigmoid(g) * u).astype(BF16)
        o_ref[...] += jnp.dot(a, wd_ref[c], preferred_element_type=F32)


def _ffn_call(x, gain, wg, wu, wd, pre=None, *, row_target=512):
    rows, d = x.shape
    n_chunks, _, tf = wg.shape
    tm = _row_tile(rows, row_target)
    row_spec = pl.BlockSpec((tm, d), lambda i: (i, 0))
    in_specs, args = [row_spec], [x]
    if pre is not None:
        a, wo = pre
        in_specs += [pl.BlockSpec((tm, a.shape[1]), lambda i: (i, 0)), _const_spec(wo.shape)]
        args += [a, wo]
    in_specs += [_const_spec((1, d)), _const_spec(wg.shape), _const_spec(wu.shape),
                 _const_spec(wd.shape)]
    args += [gain.reshape(1, d), wg, wu, wd]
    return pl.pallas_call(
        functools.partial(_ffn_kernel, has_pre=pre is not None, n_chunks=n_chunks),
        out_shape=jax.ShapeDtypeStruct((rows, d), F32),
        grid=(rows // tm,),
        in_specs=in_specs,
        out_specs=row_spec,
        scratch_shapes=[pltpu.VMEM((tm, d), BF16)],
        compiler_params=_params(("parallel",)),
        name="ffn_pre" if pre is not None else "ffn",
    )(*args)


def _ffn_weights(w_gu, w_down):
    d, two_ff = w_gu.shape
    d_ff = two_ff // 2
    tf = FFN_CHUNK
    while d_ff % tf:
        tf //= 2
    nc = d_ff // tf
    wg = w_gu[:, :d_ff].reshape(d, nc, tf).transpose(1, 0, 2).astype(BF16)
    wu = w_gu[:, d_ff:].reshape(d, nc, tf).transpose(1, 0, 2).astype(BF16)
    wd = w_down.reshape(nc, tf, d).astype(BF16)
    return wg, wu, wd


def _activate_gates(g, is_forget):
    g = GATE_SOFTCAP * jnp.tanh(g / GATE_SOFTCAP)
    log_sig = jnp.minimum(g, 0.0) - jnp.log1p(jnp.exp(-jnp.abs(g)))
    return jnp.where(is_forget, log_sig, g)


def _ml_proj_prompt_kernel(x_ref, g_ref, wqvo_ref, wkT_ref, wgT_ref, gb_ref,
                           q_ref, kT_ref, v_ref, o_ref, gates_ref):
    h = _rms_rows(x_ref[...], g_ref[...]).astype(BF16)
    z = jnp.dot(h, wqvo_ref[...], preferred_element_type=F32)
    q_ref[...] = z[:, :ML_QK_W].astype(BF16)
    v_ref[...] = z[:, ML_QK_W:ML_QK_W + ML_V_W].astype(BF16)
    o_ref[...] = z[:, ML_QK_W + ML_V_W:]
    nt = (((1,), (1,)), ((), ()))
    kT = lax.dot_general(wkT_ref[...], h, nt, preferred_element_type=F32)
    kT_ref[0] = (kT * (ML_DK ** -0.5)).astype(BF16)
    gT = lax.dot_general(wgT_ref[...], h, nt, preferred_element_type=F32) + gb_ref[...]
    row = lax.broadcasted_iota(jnp.int32, gT.shape, 0)
    gates_ref[0] = _activate_gates(gT, row >= ML_HEADS)


def _ml_proj_prompt_call(x, gain, w_in, gate_bias, batch, seq, *, row_target=512):
    rows, d = x.shape
    tm = _row_tile(seq, row_target)
    per_seq = seq // tm
    ng = 2 * ML_HEADS
    wqvo = jnp.concatenate([w_in[:, :ML_QK_W], w_in[:, 2 * ML_QK_W:2 * ML_QK_W + 2 * ML_V_W]],
                           axis=1).astype(BF16)
    wkT = w_in[:, ML_QK_W:2 * ML_QK_W].T.astype(BF16)
    wgT = w_in[:, 2 * ML_QK_W + 2 * ML_V_W:].T.astype(BF16)
    row = lambda w: pl.BlockSpec((tm, w), lambda i: (i, 0))
    by_seq = lambda r: pl.BlockSpec((1, r, tm), lambda i: (i // per_seq, 0, i % per_seq))
    return pl.pallas_call(
        _ml_proj_prompt_kernel,
        out_shape=(jax.ShapeDtypeStruct((rows, ML_QK_W), BF16),
                   jax.ShapeDtypeStruct((batch, ML_QK_W, seq), BF16),
                   jax.ShapeDtypeStruct((rows, ML_V_W), BF16),
                   jax.ShapeDtypeStruct((rows, ML_V_W), F32),
                   jax.ShapeDtypeStruct((batch, ng, seq), F32)),
        grid=(rows // tm,),
        in_specs=[row(d), _const_spec((1, d)), _const_spec(wqvo.shape), _const_spec(wkT.shape),
                  _const_spec(wgT.shape), _const_spec((ng, 1))],
        out_specs=(row(ML_QK_W), by_seq(ML_QK_W), row(ML_V_W), row(ML_V_W), by_seq(ng)),
        compiler_params=_params(("parallel",)),
        name="ml_proj_prompt",
    )(x, gain.reshape(1, d), wqvo, wkT, wgT, gate_bias.reshape(ng, 1).astype(F32))


def _ml_recur_kernel(q_ref, kT_ref, v_ref, o_ref, gates_ref, gain_ref,
                     act_ref, c_out, n_out, m_out, c_s, n_s, m_s, *, chunk):
    j = pl.program_id(1)

    @pl.when(j == 0)
    def _():
        c_s[...] = jnp.zeros_like(c_s)
        n_s[...] = jnp.zeros_like(n_s)
        m_s[...] = jnp.zeros_like(m_s)

    gates = gates_ref[0]
    lane = lax.broadcasted_iota(jnp.int32, gates.shape, 1)
    cum = gates
    step = 1
    while step < chunk:
        cum = cum + jnp.where(lane >= step, pltpu.roll(cum, step, axis=1), 0.0)
        step *= 2
    t_idx = lax.broadcasted_iota(jnp.int32, (chunk, chunk), 0)
    s_idx = lax.broadcasted_iota(jnp.int32, (chunk, chunk), 1)
    causal = s_idx <= t_idx
    nt = (((1,), (1,)), ((), ()))

    heads = range(ML_HEADS)
    dot = functools.partial(jnp.dot, preferred_element_type=F32)
    qh = [q_ref[0, :, h * ML_DK:(h + 1) * ML_DK] for h in heads]
    kTh = [kT_ref[0, h * ML_DK:(h + 1) * ML_DK, :] for h in heads]
    vh = [v_ref[0, :, h * ML_DV:(h + 1) * ML_DV] for h in heads]
    c_old = [c_s[h] for h in heads]
    n_row = [n_s[h:h + 1, :] for h in heads]
    m_prev = [m_s[h:h + 1, 0:1] for h in heads]
    qk = [dot(qh[h], kTh[h]) for h in heads]
    qc = [dot(qh[h], c_old[h].astype(BF16)) for h in heads]
    qn = [jnp.sum(qh[h].astype(F32) * n_row[h], axis=1, keepdims=True) for h in heads]

    li_r = [gates[h:h + 1, :] for h in heads]
    lf_r = [gates[ML_HEADS + h:ML_HEADS + h + 1, :] for h in heads]
    b_r = [cum[ML_HEADS + h:ML_HEADS + h + 1, :] for h in heads]
    b_last = [b[:, chunk - 1:chunk] for b in b_r]
    b_c = [jnp.sum(jnp.where(causal, lf, 0.0), axis=1, keepdims=True) for lf in lf_r]
    d_mat = [jnp.where(causal, b_c[h] - b_r[h] + li_r[h], -jnp.inf) for h in heads]
    a = [b_c[h] + m_prev[h] for h in heads]
    mt = [jnp.maximum(a[h], jnp.max(d_mat[h], axis=1, keepdims=True)) for h in heads]
    s = [qk[h] * jnp.exp(d_mat[h] - mt[h]) for h in heads]
    sv = [dot(s[h].astype(BF16), vh[h]) for h in heads]

    g_r = [b_last[h] - b_r[h] + li_r[h] for h in heads]
    m_new = [jnp.maximum(b_last[h] + m_prev[h], jnp.max(g_r[h], axis=1, keepdims=True))
             for h in heads]
    decay = [jnp.exp(b_last[h] + m_prev[h] - m_new[h]) for h in heads]
    wk = [jnp.exp(g_r[h] - m_new[h]) for h in heads]
    kv = [dot((kTh[h].astype(F32) * wk[h]).astype(BF16), vh[h]) for h in heads]
    nk = [lax.dot_general(wk[h].astype(BF16), kTh[h], nt, preferred_element_type=F32)
          for h in heads]

    for h in heads:
        w_inter = jnp.exp(a[h] - mt[h])
        num = w_inter * qc[h] + sv[h]
        den = w_inter * qn[h] + jnp.sum(s[h], axis=1, keepdims=True)
        hid = num / jnp.maximum(jnp.abs(den), jnp.exp(-mt[h]))
        hn = _rms_rows(hid, gain_ref[h:h + 1, :])
        og = jax.nn.sigmoid(o_ref[0, :, h * ML_DV:(h + 1) * ML_DV])
        act_ref[0, :, h * ML_DV:(h + 1) * ML_DV] = (hn * og).astype(BF16)
    for h in heads:
        c_s[h] = decay[h] * c_old[h] + kv[h]
        n_s[h:h + 1, :] = decay[h] * n_row[h] + nk[h]
        m_s[h:h + 1, :] = jnp.broadcast_to(m_new[h], (1, V7X_LANES))

    @pl.when(j == pl.num_programs(1) - 1)
    def _():
        c_out[0] = c_s[...]
        n_out[0] = n_s[0:ML_HEADS, :]
        lane_m = lax.broadcasted_iota(jnp.int32, (1, V7X_LANES), 1)
        m_row = jnp.zeros((1, V7X_LANES), F32)
        for h in range(ML_HEADS):
            m_row = jnp.where(lane_m == h, m_s[h:h + 1, :], m_row)
        m_out[0] = m_row


def _ml_recur_call(q, kT, v, o, gates, out_gain, batch, seq):
    chunk = _row_tile(seq, ML_CHUNK)
    n_steps = seq // chunk
    ng = 2 * ML_HEADS
    q3 = q.reshape(batch, seq, ML_QK_W)
    v3 = v.reshape(batch, seq, ML_V_W)
    o3 = o.reshape(batch, seq, ML_V_W)
    tok = lambda w: pl.BlockSpec((1, chunk, w), lambda b, j: (b, j, 0))
    col = lambda r: pl.BlockSpec((1, r, chunk), lambda b, j: (b, 0, j))
    per_b = lambda *s: pl.BlockSpec((1,) + s, lambda b, j: (b,) + (0,) * len(s))
    act, c_new, n_new, m_new = pl.pallas_call(
        functools.partial(_ml_recur_kernel, chunk=chunk),
        out_shape=(jax.ShapeDtypeStruct((batch, seq, ML_V_W), BF16),
                   jax.ShapeDtypeStruct((batch, ML_HEADS, ML_DK, ML_DV), F32),
                   jax.ShapeDtypeStruct((batch, ML_HEADS, ML_DK), F32),
                   jax.ShapeDtypeStruct((batch, 1, V7X_LANES), F32)),
        grid=(batch, n_steps),
        in_specs=[tok(ML_QK_W), col(ML_QK_W), tok(ML_V_W), tok(ML_V_W), col(ng),
                  pl.BlockSpec((ML_HEADS, ML_DV), lambda b, j: (0, 0))],
        out_specs=(tok(ML_V_W), per_b(ML_HEADS, ML_DK, ML_DV), per_b(ML_HEADS, ML_DK),
                   per_b(1, V7X_LANES)),
        scratch_shapes=[pltpu.VMEM((ML_HEADS, ML_DK, ML_DV), F32),
                        pltpu.VMEM((8, ML_DK), F32),
                        pltpu.VMEM((8, V7X_LANES), F32)],
        compiler_params=_params(("parallel", "arbitrary")),
        name="ml_recur",
    )(q3, kT, v3, o3, gates, out_gain.astype(F32))
    return act.reshape(batch * seq, ML_V_W), c_new, n_new, m_new[:, 0, :ML_HEADS]


def _ml_proj_decode_kernel(x_ref, g_ref, w_ref, wqkT_ref, wg_ref, gb_ref, z_ref, zT_ref, gates_ref):
    h = _rms_rows(x_ref[...], g_ref[...]).astype(BF16)
    z = jnp.dot(h, w_ref[...], preferred_element_type=F32)
    col = lax.broadcasted_iota(jnp.int32, z.shape, 1)
    is_k = (col >= ML_QK_W) & (col < 2 * ML_QK_W)
    z_ref[...] = jnp.where(is_k, z * (ML_DK ** -0.5), z)
    nt = (((1,), (1,)), ((), ()))
    zT = lax.dot_general(wqkT_ref[...], h, nt, preferred_element_type=F32)
    rowi = lax.broadcasted_iota(jnp.int32, zT.shape, 0)
    zT_ref[...] = jnp.where(rowi >= ML_QK_W, zT * (ML_DK ** -0.5), zT)
    g = jnp.dot(h, wg_ref[...], preferred_element_type=F32) + gb_ref[...]
    lane = lax.broadcasted_iota(jnp.int32, g.shape, 1)
    gates_ref[...] = _activate_gates(g, lane >= ML_HEADS)


def _ml_proj_decode_call(x, gain, w_in, gate_bias):
    rows, d = x.shape
    ng = 2 * ML_HEADS
    w_main = w_in[:, :2 * ML_QK_W + 2 * ML_V_W].astype(BF16)
    wqkT = w_in[:, :2 * ML_QK_W].T.astype(BF16)
    wg = jnp.pad(w_in[:, 2 * ML_QK_W + 2 * ML_V_W:], ((0, 0), (0, V7X_LANES - ng))).astype(BF16)
    gb = jnp.pad(gate_bias.astype(F32), (0, V7X_LANES - ng)).reshape(1, V7X_LANES)
    full = lambda a: _const_spec(a.shape)
    args = (x, gain.reshape(1, d), w_main, wqkT, wg, gb)
    out_shape = (jax.ShapeDtypeStruct((rows, w_main.shape[1]), F32),
                 jax.ShapeDtypeStruct((2 * ML_QK_W, rows), F32),
                 jax.ShapeDtypeStruct((rows, V7X_LANES), F32))
    return pl.pallas_call(
        _ml_proj_decode_kernel,
        out_shape=out_shape,
        grid=(1,),
        in_specs=[full(a) for a in args],
        out_specs=tuple(pl.BlockSpec(s.shape, lambda i: (0, 0)) for s in out_shape),
        compiler_params=_params(("arbitrary",)),
        name="ml_proj_decode",
    )(*args)


def _ml_decode_kernel(z_ref, zT_ref, gates_ref, c_ref, n_ref, m_ref, gain_ref,
                      act_ref, c_out, n_out, m_out):
    b = pl.program_id(0)
    z = z_ref[pl.ds(b, 1), :]
    g = gates_ref[pl.ds(b, 1), :]
    m_in = m_ref[pl.ds(b, 1), :]
    zT = zT_ref[...]
    pick = lax.broadcasted_iota(jnp.int32, zT.shape, 1) == b
    z_col = jnp.sum(jnp.where(pick, zT, 0.0), axis=1, keepdims=True)
    lane_m = lax.broadcasted_iota(jnp.int32, (1, V7X_LANES), 1)
    m_row = jnp.zeros((1, V7X_LANES), F32)

    for h in range(ML_HEADS):
        li = g[:, h:h + 1]
        lf = g[:, ML_HEADS + h:ML_HEADS + h + 1]
        m_prev = m_in[:, h:h + 1]
        q_row = z[:, h * ML_DK:(h + 1) * ML_DK]
        k_row = z[:, ML_QK_W + h * ML_DK:ML_QK_W + (h + 1) * ML_DK]
        v_row = z[:, 2 * ML_QK_W + h * ML_DV:2 * ML_QK_W + (h + 1) * ML_DV]
        o_row = z[:, 2 * ML_QK_W + ML_V_W + h * ML_DV:2 * ML_QK_W + ML_V_W + (h + 1) * ML_DV]
        q_col = z_col[h * ML_DK:(h + 1) * ML_DK, :]
        k_col = z_col[ML_QK_W + h * ML_DK:ML_QK_W + (h + 1) * ML_DK, :]
        c_old = c_ref[0, h]
        n_row = n_ref[0, h:h + 1, :]

        a = lf + m_prev
        mt = jnp.maximum(a, li)
        w_inter = jnp.exp(a - mt)
        s = jnp.sum(q_row * k_row, axis=1, keepdims=True) * jnp.exp(li - mt)
        num = w_inter * jnp.sum(q_col * c_old, axis=0, keepdims=True) + s * v_row
        den = w_inter * jnp.sum(q_row * n_row, axis=1, keepdims=True) + s
        hid = num / jnp.maximum(jnp.abs(den), jnp.exp(-mt))
        hn = _rms_rows(hid, gain_ref[h:h + 1, :])
        act_ref[0, :, h * ML_DV:(h + 1) * ML_DV] = (hn * jax.nn.sigmoid(o_row)).astype(BF16)

        m_new = jnp.maximum(a, li)
        decay = jnp.exp(a - m_new)
        wk = jnp.exp(li - m_new)
        c_out[0, h] = decay * c_old + (wk * k_col) * v_row
        n_out[0, h:h + 1, :] = decay * n_row + wk * k_row
        m_row = jnp.where(lane_m == h, m_new, m_row)
    m_out[0] = m_row


def _ml_decode_call(z, zT, gates, state_c, state_n, state_m, out_gain):
    rows = z.shape[0]
    full = lambda a: _const_spec(a.shape)
    per_b = lambda *s: pl.BlockSpec((1,) + s, lambda b: (b,) + (0,) * len(s))
    act, c_new, n_new, m_new = pl.pallas_call(
        _ml_decode_kernel,
        out_shape=(jax.ShapeDtypeStruct((rows, 1, ML_V_W), BF16),
                   jax.ShapeDtypeStruct(state_c.shape, F32),
                   jax.ShapeDtypeStruct(state_n.shape, F32),
                   jax.ShapeDtypeStruct((rows, 1, V7X_LANES), F32)),
        grid=(rows,),
        in_specs=[full(z), full(zT), full(gates), per_b(ML_HEADS, ML_DK, ML_DV),
                  per_b(ML_HEADS, ML_DK), full(state_m), full(out_gain)],
        out_specs=(per_b(1, ML_V_W), per_b(ML_HEADS, ML_DK, ML_DV), per_b(ML_HEADS, ML_DK),
                   per_b(1, V7X_LANES)),
        compiler_params=_params(("arbitrary",)),
        name="ml_decode",
    )(z, zT, gates, state_c, state_n, state_m, out_gain.astype(F32))
    return act.reshape(rows, ML_V_W), c_new, n_new, m_new[:, 0, :ML_HEADS]


def _subhead_mean_square(t, bsum):
    t2 = t * t
    hi = t2.astype(BF16)
    lo = (t2 - hi.astype(F32)).astype(BF16)
    parts = []
    for j in range(t.shape[1] // V7X_MXU_DIM):
        sl = slice(j * V7X_MXU_DIM, (j + 1) * V7X_MXU_DIM)
        parts.append(jnp.dot(hi[:, sl], bsum, preferred_element_type=F32)
                     + jnp.dot(lo[:, sl], bsum, preferred_element_type=F32))
    return jnp.concatenate(parts, axis=1)


def _rope(t, cos, sin_signed):
    lane = lax.broadcasted_iota(jnp.int32, (t.shape[0], V7X_LANES), 1)
    first = (lane % DA_HEAD_DIM) < ROT_HALF
    parts = []
    for j in range(t.shape[1] // V7X_LANES):
        tj = t[:, j * V7X_LANES:(j + 1) * V7X_LANES]
        partner = jnp.where(first, pltpu.roll(tj, V7X_LANES - ROT_HALF, axis=1),
                            pltpu.roll(tj, ROT_HALF, axis=1))
        parts.append(tj * cos + partner * sin_signed)
    return jnp.concatenate(parts, axis=1)


def _da_proj_kernel(*refs, for_prompt):
    if for_prompt:
        (x_ref, g_ref, w_ref, qg_ref, kg_ref, bsum_ref, cos_ref, sin_ref, wvT_ref,
         q_ref, k_ref, v_ref, kb_ref, vT_ref) = refs
    else:
        (x_ref, g_ref, w_ref, qg_ref, kg_ref, bsum_ref, cos_ref, sin_ref,
         q_ref, k_ref, v_ref) = refs
    h = _rms_rows(x_ref[...], g_ref[...]).astype(BF16)
    z = jnp.dot(h, w_ref[...], preferred_element_type=F32)
    bsum = bsum_ref[...]
    cos = cos_ref[...]
    sin = sin_ref[...]
    q = z[:, :DA_W]
    k = z[:, DA_W:2 * DA_W]
    q = _rope(q * lax.rsqrt(_subhead_mean_square(q, bsum) + EPS) * qg_ref[...], cos, sin)
    k = _rope(k * lax.rsqrt(_subhead_mean_square(k, bsum) + EPS) * kg_ref[...], cos, sin)
    q_ref[...] = (q * SCORE_SCALE_LOG2).astype(BF16)
    k_ref[...] = k
    v_ref[...] = z[:, 2 * DA_W:]
    if for_prompt:
        kb_ref[...] = k.astype(BF16)
        nt = (((1,), (1,)), ((), ()))
        vT_ref[0] = lax.dot_general(wvT_ref[...], h, nt, preferred_element_type=F32).astype(BF16)


def _rope_tables(pos):
    inv_freq = jnp.power(ROPE_THETA, -jnp.arange(0, ROT_DIM, 2, dtype=F32) / ROT_DIM)
    ang = pos.astype(F32)[:, None] * inv_freq[None, :]
    cos, sin = jnp.cos(ang), jnp.sin(ang)
    ones = jnp.ones((pos.shape[0], DA_HEAD_DIM - ROT_DIM), F32)
    cos64 = jnp.concatenate([cos, cos, ones], axis=1)
    sin64 = jnp.concatenate([-sin, sin, 0.0 * ones], axis=1)
    return jnp.tile(cos64, (1, 2)), jnp.tile(sin64, (1, 2))


def _da_proj_call(x, gain, w_qkv, q_gain, k_gain, pos, seq, *, for_prompt, row_target=512):
    rows, d = x.shape
    tm = _row_tile(seq, row_target)
    per_seq = seq // tm
    w = w_qkv.astype(BF16)
    cos, sin = _rope_tables(pos)
    blk = jnp.arange(V7X_MXU_DIM) // DA_HEAD_DIM
    bsum = jnp.where(blk[:, None] == blk[None, :], 1.0 / DA_HEAD_DIM, 0.0).astype(BF16)
    qg = jnp.tile(q_gain.astype(F32), DA_SUB).reshape(1, DA_W)
    kg = jnp.tile(k_gain.astype(F32), DA_SUB).reshape(1, DA_W)
    row = lambda wd: pl.BlockSpec((tm, wd), lambda i: (i, 0))
    table = pl.BlockSpec((tm, V7X_LANES), lambda i: (i % per_seq, 0))
    n_v = w.shape[1] - 2 * DA_W
    in_specs = [row(d), _const_spec((1, d)), _const_spec(w.shape), _const_spec((1, DA_W)),
                _const_spec((1, DA_W)), _const_spec(bsum.shape), table, table]
    args = [x, gain.reshape(1, d), w, qg, kg, bsum, cos, sin]
    out_shape = [jax.ShapeDtypeStruct((rows, DA_W), BF16), jax.ShapeDtypeStruct((rows, DA_W), F32),
                 jax.ShapeDtypeStruct((rows, n_v), F32)]
    out_specs = [row(DA_W), row(DA_W), row(n_v)]
    if for_prompt:
        wvT = w_qkv[:, 2 * DA_W:].T.astype(BF16)
        in_specs.append(_const_spec(wvT.shape))
        args.append(wvT)
        out_shape += [jax.ShapeDtypeStruct((rows, DA_W), BF16),
                      jax.ShapeDtypeStruct((rows // seq, n_v, seq), BF16)]
        out_specs += [row(DA_W),
                      pl.BlockSpec((1, n_v, tm), lambda i: (i // per_seq, 0, i % per_seq))]
    return pl.pallas_call(
        functools.partial(_da_proj_kernel, for_prompt=for_prompt),
        out_shape=tuple(out_shape),
        grid=(rows // tm,),
        in_specs=in_specs,
        out_specs=tuple(out_specs),
        compiler_params=_params(("parallel",)),
        name="da_proj_prompt" if for_prompt else "da_proj_decode",
    )(*args)


def _lambda_value(lam_ref, lam_init):
    lp = lam_ref[...]
    s1 = jnp.sum(lp[0:1, :] * lp[1:2, :], axis=1, keepdims=True)
    s2 = jnp.sum(lp[2:3, :] * lp[3:4, :], axis=1, keepdims=True)
    return jnp.exp(s1) - jnp.exp(s2) + lam_init


def _da_attn_kernel(q_ref, k_ref, vT_ref, lam_ref, subln_ref, o_ref, q2_s, m_s, l_s, acc_s,
                    *, tq, tk, lam_init):
    i = pl.program_id(1)
    nt = (((1,), (1,)), ((), ()))
    lane = lax.broadcasted_iota(jnp.int32, (tq, V7X_LANES), 1)
    for h in range(DA_HEADS):
        qp = q_ref[0, :, h * DA_VDIM:(h + 1) * DA_VDIM]
        zero = jnp.zeros_like(qp)
        q2_s[h, 0:tq, :] = jnp.where(lane < DA_HEAD_DIM, qp, zero)
        q2_s[h, tq:2 * tq, :] = jnp.where(lane >= DA_HEAD_DIM, qp, zero)
    m_s[...] = jnp.full(m_s.shape, -jnp.inf, F32)
    l_s[...] = jnp.zeros_like(l_s)
    acc_s[...] = jnp.zeros_like(acc_s)

    key_l = lax.broadcasted_iota(jnp.int32, (tk, 2 * tq), 0)
    qry_g = i * tq + lax.broadcasted_iota(jnp.int32, (tk, 2 * tq), 1) % tq

    def block(j, masked):
        start = pl.multiple_of(j * tk, tk)

        def scores(h):
            kb = k_ref[0, pl.ds(start, tk), h * DA_VDIM:(h + 1) * DA_VDIM]
            return lax.dot_general(kb, q2_s[h], nt, preferred_element_type=F32)

        ahead = [scores(h) for h in range(ATTN_HEADS_AHEAD)]
        for h in range(DA_HEADS):
            hs = slice(h * DA_VDIM, (h + 1) * DA_VDIM)
            s = ahead.pop(0)
            if h + ATTN_HEADS_AHEAD < DA_HEADS:
                ahead.append(scores(h + ATTN_HEADS_AHEAD))
            if masked:
                s = jnp.where(j * tk + key_l <= qry_g, s, -jnp.inf)
            m_old = m_s[h]
            m_new = jnp.maximum(m_old, jnp.max(s, axis=0, keepdims=True))
            alpha = jnp.exp2(m_old - m_new)
            p = jnp.exp2(s - m_new)
            l_s[h] = alpha * l_s[h] + jnp.sum(p, axis=0, keepdims=True)
            vTb = vT_ref[0, hs, pl.ds(start, tk)]
            acc_s[h] = alpha * acc_s[h] + jnp.dot(vTb, p.astype(BF16),
                                                  preferred_element_type=F32)
            m_s[h] = m_new

    n_full = (i * tq) // tk

    def full_block(j, carry):
        block(j, False)
        return carry

    lax.fori_loop(0, n_full, full_block, 0)
    block(n_full, True)

    lam = _lambda_value(lam_ref, lam_init)
    for h in range(DA_HEADS):
        oT = acc_s[h] / l_s[h]
        oT = oT[:, :tq] - lam * oT[:, tq:]
        ms = jnp.mean(oT * oT, axis=0, keepdims=True)
        oT = oT * lax.rsqrt(ms + EPS) * (subln_ref[...] * (1.0 - lam_init))
        o_ref[0, :, h * DA_VDIM:(h + 1) * DA_VDIM] = oT.T.astype(BF16)


def _da_attn_call(q, kb, vT, lam_p, subln, batch, seq, lam_init):
    tk = _row_tile(seq, ATTN_KV_BLOCK)
    tq = min(_row_tile(seq, ATTN_Q_BLOCK), tk)
    assert tk % tq == 0
    w = q.shape[1]
    q3, k3 = (t.reshape(batch, seq, t.shape[1]) for t in (q, kb))
    out = pl.pallas_call(
        functools.partial(_da_attn_kernel, tq=tq, tk=tk, lam_init=lam_init),
        out_shape=jax.ShapeDtypeStruct((batch, seq, vT.shape[1]), BF16),
        grid=(batch, seq // tq),
        in_specs=[pl.BlockSpec((1, tq, w), lambda b, i: (b, i, 0)),
                  pl.BlockSpec((1, seq, w), lambda b, i: (b, 0, 0)),
                  pl.BlockSpec((1, vT.shape[1], seq), lambda b, i: (b, 0, 0)),
                  pl.BlockSpec(lam_p.shape, lambda b, i: (0, 0)),
                  pl.BlockSpec((DA_VDIM, 1), lambda b, i: (0, 0))],
        out_specs=pl.BlockSpec((1, tq, vT.shape[1]), lambda b, i: (b, i, 0)),
        scratch_shapes=[pltpu.VMEM((DA_HEADS, 2 * tq, DA_VDIM), BF16),
                        pltpu.VMEM((DA_HEADS, 1, 2 * tq), F32),
                        pltpu.VMEM((DA_HEADS, 1, 2 * tq), F32),
                        pltpu.VMEM((DA_HEADS, DA_VDIM, 2 * tq), F32)],
        compiler_params=_params(("parallel", "arbitrary")),
        name="da_attn_prompt",
    )(q3, k3, vT, lam_p.astype(F32), subln.reshape(DA_VDIM, 1).astype(F32))
    return out.reshape(batch * seq, vT.shape[1])


def _da_decode_kernel(pt_ref, q_ref, kn_ref, vn_ref, lam_ref, subln_ref, expand_ref, *rest,
                      pages, lam_init):
    k_refs = rest[:pages]
    v_refs = rest[pages:2 * pages]
    o_ref, qbd_s, m_s, l_s, acc_s = rest[2 * pages:]
    j = pl.program_id(1)
    flat = (DA_SUB * DA_HEAD_DIM, PAGE_SIZE)

    @pl.when(j == 0)
    def _():
        q = q_ref[0]
        sub = lax.broadcasted_iota(jnp.int32, (DA_SUB, DA_W), 0)
        col = lax.broadcasted_iota(jnp.int32, (DA_SUB, DA_W), 1)
        qb = jnp.broadcast_to(q.astype(F32), (DA_SUB, DA_W))
        qbd_s[...] = jnp.where(col // DA_HEAD_DIM == sub, qb, 0.0).astype(BF16)
        m_s[...] = jnp.full(m_s.shape, -jnp.inf, F32)
        l_s[...] = jnp.zeros_like(l_s)
        acc_s[...] = jnp.zeros_like(acc_s)

    qbd = qbd_s[...]
    s = jnp.concatenate(
        [jnp.dot(qbd, k_refs[i][...].reshape(flat).astype(BF16), preferred_element_type=F32)
         for i in range(pages)], axis=1)
    m_old = m_s[...]
    m_new = jnp.maximum(m_old, jnp.max(s, axis=1, keepdims=True))
    alpha = jnp.exp2(m_old - m_new)
    p = jnp.exp2(s - m_new)
    l_s[...] = alpha * l_s[...] + jnp.sum(p, axis=1, keepdims=True)
    p_rows = jnp.concatenate([p[:, i * PAGE_SIZE:(i + 1) * PAGE_SIZE] for i in range(pages)],
                             axis=0).astype(BF16)
    w = jnp.dot(p_rows, expand_ref[...], preferred_element_type=F32)
    sub = lax.broadcasted_iota(jnp.int32, w.shape, 0) % DA_SUB
    head = lax.broadcasted_iota(jnp.int32, w.shape, 1) % DA_HEADS
    w = jnp.where(head == sub // 2, w, 0.0).astype(BF16)
    pv = jnp.zeros(acc_s.shape, F32)
    for i in range(pages):
        pv += jnp.dot(w[i * DA_SUB:(i + 1) * DA_SUB], v_refs[i][...].reshape(flat).astype(BF16),
                      preferred_element_type=F32)
    acc_s[...] = alpha * acc_s[...] + pv
    m_s[...] = m_new

    @pl.when(j == pl.num_programs(1) - 1)
    def _():
        lam = _lambda_value(lam_ref, lam_init)
        m_p = m_s[...]
        s_self = jnp.sum(qbd.astype(F32) * kn_ref[0], axis=1, keepdims=True)
        m_f = jnp.maximum(m_p, s_self)
        a_f = jnp.exp2(m_p - m_f)
        p_self = jnp.exp2(s_self - m_f)
        l_f = a_f * l_s[...] + p_self
        past = a_f * acc_s[...]
        v_self = vn_ref[0]
        for h in range(DA_HEADS):
            o2 = ((past[2 * h:2 * h + 2] + p_self[2 * h:2 * h + 2] * v_self[h:h + 1])
                  / l_f[2 * h:2 * h + 2])
            oh = o2[0:1] - lam * o2[1:2]
            oh = _rms_rows(oh, subln_ref[...]) * (1.0 - lam_init)
            o_ref[0, h:h + 1, :] = oh.astype(BF16)


def _da_decode_call(q, k_new, v_new, cache_k, cache_v, layer, page_table, lam_p, subln, lam_init):
    rows = q.shape[0]
    n_pages = page_table.shape[1]
    pages = DECODE_PAGES_PER_STEP
    while n_pages % pages:
        pages //= 2
    ck = jnp.transpose(cache_k, (0, 1, 3, 4, 2))
    per_b = lambda a: pl.BlockSpec((1,) + a.shape[1:], lambda b, j, pt: (b, 0, 0))
    page_index = lambda i: (lambda b, j, pt: (layer, pt[b, j * pages + i], 0, 0, 0))
    k_spec = lambda i: pl.BlockSpec((None, None, DA_SUB, DA_HEAD_DIM, PAGE_SIZE), page_index(i))
    v_spec = lambda i: pl.BlockSpec((None, None, PAGE_SIZE, DA_HEADS, DA_VDIM), page_index(i))
    token = jnp.arange(PAGE_SIZE * DA_HEADS) // DA_HEADS
    expand = (token[None, :] == jnp.arange(PAGE_SIZE)[:, None]).astype(BF16)

    q3 = q.reshape(rows, 1, DA_W)
    kn3 = k_new.reshape(rows, 1, DA_W)
    vn3 = v_new.reshape(rows, DA_HEADS, DA_VDIM)
    grid_spec = pltpu.PrefetchScalarGridSpec(
        num_scalar_prefetch=1,
        grid=(rows, n_pages // pages),
        in_specs=[per_b(q3), per_b(kn3), per_b(vn3),
                  pl.BlockSpec(lam_p.shape, lambda b, j, pt: (0, 0)),
                  pl.BlockSpec((1, DA_VDIM), lambda b, j, pt: (0, 0)),
                  pl.BlockSpec(expand.shape, lambda b, j, pt: (0, 0))]
        + [k_spec(i) for i in range(pages)] + [v_spec(i) for i in range(pages)],
        out_specs=pl.BlockSpec((1, DA_HEADS, DA_VDIM), lambda b, j, pt: (b, 0, 0)),
        scratch_shapes=[pltpu.VMEM((DA_SUB, DA_W), BF16), pltpu.VMEM((DA_SUB, 1), F32),
                        pltpu.VMEM((DA_SUB, 1), F32), pltpu.VMEM((DA_SUB, DA_VDIM), F32)],
    )
    out = pl.pallas_call(
        functools.partial(_da_decode_kernel, pages=pages, lam_init=lam_init),
        out_shape=jax.ShapeDtypeStruct((rows, DA_HEADS, DA_VDIM), BF16),
        grid_spec=grid_spec,
        compiler_params=_params(("parallel", "arbitrary")),
        name="da_attn_decode",
    )(page_table, q3, kn3, vn3, lam_p.astype(F32), subln.reshape(1, DA_VDIM).astype(F32), expand,
      *([ck] * pages), *([cache_v] * pages))
    return out.reshape(rows, DA_HEADS * DA_VDIM)


def kernel(x_prompt, x_sample, state_C, state_n, state_m, cache_k, cache_v, page_table,
           ffn1_norm, ffn1_w_gu, ffn1_w_down, mix_norm, ffn2_norm, ffn2_w_gu, ffn2_w_down,
           ml_w_in, ml_gate_bias, ml_out_norm, ml_w_out,
           da_w_qkv, da_q_norm, da_k_norm, da_lambda, da_subln, da_w_out):
    depth = ffn1_norm.shape[0]
    bp, lp, d = x_prompt.shape
    bs, ls, _ = x_sample.shape
    assert ls == 1
    past_len = page_table.shape[1] * PAGE_SIZE
    pos_p = jnp.arange(lp, dtype=jnp.int32)
    pos_s = past_len + jnp.arange(ls, dtype=jnp.int32)

    xp = x_prompt.reshape(bp * lp, d)
    xs = x_sample.reshape(bs * ls, d)
    ml_p, ml_s, kv_p, kv_s = [], [], [], []
    for i in range(depth):
        j = i // N_MIXERS
        w1 = _ffn_weights(ffn1_w_gu[i], ffn1_w_down[i])
        w2 = _ffn_weights(ffn2_w_gu[i], ffn2_w_down[i])
        xp = _ffn_call(xp, ffn1_norm[i], *w1)
        xs = _ffn_call(xs, ffn1_norm[i], *w1)
        if i % N_MIXERS == 0:
            q, kT, v, o, gates = _ml_proj_prompt_call(xp, mix_norm[i], ml_w_in[j], ml_gate_bias[j],
                                                      bp, lp)
            act_p, c_p, n_p, m_p = _ml_recur_call(q, kT, v, o, gates, ml_out_norm[j], bp, lp)
            ml_p.append((c_p, n_p, m_p))
            z, zT, gts = _ml_proj_decode_call(xs, mix_norm[i], ml_w_in[j], ml_gate_bias[j])
            act_s, c_s, n_s, m_s = _ml_decode_call(z, zT, gts, state_C[j], state_n[j], state_m[j],
                                                   ml_out_norm[j])
            ml_s.append((c_s, n_s, m_s))
            w_out = ml_w_out[j].astype(BF16)
        else:
            lam_init = _lambda_init(i)
            qp, kp, vp, kpb, vpT = _da_proj_call(xp, mix_norm[i], da_w_qkv[j], da_q_norm[j],
                                                 da_k_norm[j], pos_p, lp, for_prompt=True)
            act_p = _da_attn_call(qp, kpb, vpT, da_lambda[j], da_subln[j], bp, lp, lam_init)
            kv_p.append((kp.reshape(bp, lp, DA_SUB, DA_HEAD_DIM),
                         vp.reshape(bp, lp, DA_HEADS, DA_VDIM)))
            qs, ks, vs = _da_proj_call(xs, mix_norm[i], da_w_qkv[j], da_q_norm[j], da_k_norm[j],
                                       jnp.repeat(pos_s, bs), bs, for_prompt=False)
            act_s = _da_decode_call(qs, ks, vs, cache_k, cache_v, j, page_table,
                                    da_lambda[j], da_subln[j], lam_init)
            kv_s.append((ks.reshape(bs, ls, DA_SUB, DA_HEAD_DIM),
                         vs.reshape(bs, ls, DA_HEADS, DA_VDIM)))
            w_out = da_w_out[j].astype(BF16)
        xp = _ffn_call(xp, ffn2_norm[i], *w2, pre=(act_p, w_out))
        xs = _ffn_call(xs, ffn2_norm[i], *w2, pre=(act_s, w_out))

    stack = lambda items, k: jnp.stack([it[k] for it in items])
    return (xp.reshape(bp, lp, d), xs.reshape(bs, ls, d),
            stack(ml_p, 0), stack(ml_p, 1), stack(ml_p, 2),
            stack(ml_s, 0), stack(ml_s, 1), stack(ml_s, 2),
            stack(kv_p, 0), stack(kv_p, 1), stack(kv_s, 0), stack(kv_s, 1))
```

```python
import functools
import math

import jax
import jax.numpy as jnp
from jax import lax
from jax.experimental import pallas as pl
from jax.experimental.pallas import tpu as pltpu

F32 = jnp.float32
BF16 = jnp.bfloat16

EPS = 1e-6
ML_HEADS = 4
ML_DK = 128
ML_DV = 256
ML_QK_W = ML_HEADS * ML_DK
ML_V_W = ML_HEADS * ML_DV
GATE_SOFTCAP = 15.0
DA_HEAD_DIM = 64
DA_HEADS = 8
DA_SUB = 2 * DA_HEADS
DA_VDIM = 128
DA_W = DA_SUB * DA_HEAD_DIM
ROPE_THETA = 500000.0
ROT_DIM = DA_HEAD_DIM // 4
ROT_HALF = ROT_DIM // 2
PAGE_SIZE = 128
N_MIXERS = 2
SCORE_SCALE_LOG2 = (DA_HEAD_DIM ** -0.5) * math.log2(math.e)

V7X_LANES = 128
V7X_MXU_DIM = 256
V7X_VMEM_LIMIT_BYTES = 56 * 1024 * 1024

FFN_CHUNK = 256
ML_CHUNK = 128
ATTN_Q_BLOCK = 256
ATTN_KV_BLOCK = 256
ATTN_HEADS_AHEAD = 2
DECODE_PAGES_PER_STEP = 8
ML_DECODE_SEQS_PER_STEP = 4


def _lambda_init(layer):
    return 0.8 - 0.6 * math.exp(-0.3 * layer)


def _const_spec(shape):
    zeros = (0,) * len(shape)
    return pl.BlockSpec(shape, lambda *_: zeros, pipeline_mode=pl.Buffered(1))


def _params(semantics):
    return pltpu.CompilerParams(dimension_semantics=semantics,
                                vmem_limit_bytes=V7X_VMEM_LIMIT_BYTES)


def _row_tile(rows, target):
    t = min(rows, target)
    while rows % t:
        t //= 2
    return t


def _rms_rows(x, gain):
    ms = jnp.mean(x * x, axis=-1, keepdims=True)
    return x * lax.rsqrt(ms + EPS) * gain


def _ffn_kernel(*refs, has_pre, d_ff, tf):
    if has_pre:
        x_ref, a_ref, xs_ref, as_ref, wo_ref, g_ref, wgu_ref, wd_ref, o_ref, os_ref, h_scr = refs
    else:
        x_ref, xs_ref, g_ref, wgu_ref, wd_ref, o_ref, os_ref, h_scr = refs
        a_ref = as_ref = None

    def run(x_in, a_in, out):
        rows = x_in.shape[0]
        x = x_in[...]
        if has_pre:
            x = x + jnp.dot(a_in[...], wo_ref[...], preferred_element_type=F32)
        h_scr[0:rows, :] = _rms_rows(x, g_ref[...]).astype(BF16)
        out[...] = x
        for c in range(d_ff // tf):
            h = h_scr[0:rows, :]
            g = jnp.dot(h, wgu_ref[:, c * tf:(c + 1) * tf], preferred_element_type=F32)
            u = jnp.dot(h, wgu_ref[:, d_ff + c * tf:d_ff + (c + 1) * tf],
                        preferred_element_type=F32)
            act = (0.5 * g * jax.nn.sigmoid(g) * u).astype(BF16)
            out[...] += jnp.dot(act, wd_ref[c * tf:(c + 1) * tf, :], preferred_element_type=F32)

    run(x_ref, a_ref, o_ref)

    @pl.when(pl.program_id(0) == pl.num_programs(0) - 1)
    def _():
        run(xs_ref, as_ref, os_ref)


def _ffn_call(x, xs, gains, w_gu, w_down, layer, pre=None, *, row_target=512):
    rows, d = x.shape
    d_ff = w_down.shape[1]
    tf = FFN_CHUNK
    while d_ff % tf:
        tf //= 2
    tm = _row_tile(rows, row_target)
    assert xs.shape[0] <= tm
    row_spec = pl.BlockSpec((tm, d), lambda i: (i, 0))
    whole = lambda a: _const_spec(a.shape)
    of_layer = lambda a: pl.BlockSpec((None,) + a.shape[1:], lambda i: (layer, 0, 0),
                                      pipeline_mode=pl.Buffered(1))
    in_specs, args = [row_spec], [x]
    if pre is not None:
        a, a_s, wo = pre
        in_specs += [pl.BlockSpec((tm, a.shape[1]), lambda i: (i, 0)), whole(xs), whole(a_s),
                     whole(wo)]
        args += [a, xs, a_s, wo]
    else:
        in_specs.append(whole(xs))
        args.append(xs)
    in_specs += [of_layer(gains), of_layer(w_gu), of_layer(w_down)]
    args += [gains, w_gu, w_down]
    return pl.pallas_call(
        functools.partial(_ffn_kernel, has_pre=pre is not None, d_ff=d_ff, tf=tf),
        out_shape=(jax.ShapeDtypeStruct((rows, d), F32), jax.ShapeDtypeStruct(xs.shape, F32)),
        grid=(rows // tm,),
        in_specs=in_specs,
        out_specs=(row_spec, pl.BlockSpec(xs.shape, lambda i: (0, 0))),
        scratch_shapes=[pltpu.VMEM((tm, d), BF16)],
        compiler_params=_params(("arbitrary",)),
        name="ffn_pre" if pre is not None else "ffn",
    )(*args)


def _activate_gates(g, is_forget):
    g = GATE_SOFTCAP * jnp.tanh(g / GATE_SOFTCAP)
    log_sig = jnp.minimum(g, 0.0) - jnp.log1p(jnp.exp(-jnp.abs(g)))
    return jnp.where(is_forget, log_sig, g)


def _ml_proj_prompt_kernel(x_ref, g_ref, wqvo_ref, wkT_ref, wgT_ref, gb_ref,
                           q_ref, kT_ref, v_ref, o_ref, gates_ref):
    h = _rms_rows(x_ref[...], g_ref[...]).astype(BF16)
    z = jnp.dot(h, wqvo_ref[...], preferred_element_type=F32)
    q_ref[...] = z[:, :ML_QK_W].astype(BF16)
    v_ref[...] = z[:, ML_QK_W:ML_QK_W + ML_V_W].astype(BF16)
    o_ref[...] = z[:, ML_QK_W + ML_V_W:]
    nt = (((1,), (1,)), ((), ()))
    kT = lax.dot_general(wkT_ref[...], h, nt, preferred_element_type=F32)
    kT_ref[0] = (kT * (ML_DK ** -0.5)).astype(BF16)
    gT = lax.dot_general(wgT_ref[...], h, nt, preferred_element_type=F32) + gb_ref[...]
    row = lax.broadcasted_iota(jnp.int32, gT.shape, 0)
    gates_ref[0] = _activate_gates(gT, row >= ML_HEADS)


def _ml_proj_prompt_call(x, gain, w_in, gate_bias, batch, seq, *, row_target=512):
    rows, d = x.shape
    tm = _row_tile(seq, row_target)
    per_seq = seq // tm
    ng = 2 * ML_HEADS
    wqvo = jnp.concatenate([w_in[:, :ML_QK_W], w_in[:, 2 * ML_QK_W:2 * ML_QK_W + 2 * ML_V_W]],
                           axis=1).astype(BF16)
    wkT = w_in[:, ML_QK_W:2 * ML_QK_W].T.astype(BF16)
    wgT = w_in[:, 2 * ML_QK_W + 2 * ML_V_W:].T.astype(BF16)
    row = lambda w: pl.BlockSpec((tm, w), lambda i: (i, 0))
    by_seq = lambda r: pl.BlockSpec((1, r, tm), lambda i: (i // per_seq, 0, i % per_seq))
    return pl.pallas_call(
        _ml_proj_prompt_kernel,
        out_shape=(jax.ShapeDtypeStruct((rows, ML_QK_W), BF16),
                   jax.ShapeDtypeStruct((batch, ML_QK_W, seq), BF16),
                   jax.ShapeDtypeStruct((rows, ML_V_W), BF16),
                   jax.ShapeDtypeStruct((rows, ML_V_W), F32),
                   jax.ShapeDtypeStruct((batch, ng, seq), F32)),
        grid=(rows // tm,),
        in_specs=[row(d), _const_spec((1, d)), _const_spec(wqvo.shape), _const_spec(wkT.shape),
                  _const_spec(wgT.shape), _const_spec((ng, 1))],
        out_specs=(row(ML_QK_W), by_seq(ML_QK_W), row(ML_V_W), row(ML_V_W), by_seq(ng)),
        compiler_params=_params(("parallel",)),
        name="ml_proj_prompt",
    )(x, gain.reshape(1, d), wqvo, wkT, wgT, gate_bias.reshape(ng, 1).astype(F32))


def _ml_recur_kernel(q_ref, kT_ref, v_ref, o_ref, gates_ref, gain_ref,
                     act_ref, c_out, n_out, m_out, c_s, n_s, m_s, *, chunk):
    j = pl.program_id(1)

    @pl.when(j == 0)
    def _():
        c_s[...] = jnp.zeros_like(c_s)
        n_s[...] = jnp.zeros_like(n_s)
        m_s[...] = jnp.zeros_like(m_s)

    gates = gates_ref[0]
    lane = lax.broadcasted_iota(jnp.int32, gates.shape, 1)
    cum = gates
    step = 1
    while step < chunk:
        cum = cum + jnp.where(lane >= step, pltpu.roll(cum, step, axis=1), 0.0)
        step *= 2
    t_idx = lax.broadcasted_iota(jnp.int32, (chunk, chunk), 0)
    s_idx = lax.broadcasted_iota(jnp.int32, (chunk, chunk), 1)
    causal = s_idx <= t_idx
    nt = (((1,), (1,)), ((), ()))

    heads = range(ML_HEADS)
    dot = functools.partial(jnp.dot, preferred_element_type=F32)
    qh = [q_ref[0, :, h * ML_DK:(h + 1) * ML_DK] for h in heads]
    kTh = [kT_ref[0, h * ML_DK:(h + 1) * ML_DK, :] for h in heads]
    vh = [v_ref[0, :, h * ML_DV:(h + 1) * ML_DV] for h in heads]
    c_old = [c_s[h] for h in heads]
    n_row = [n_s[h:h + 1, :] for h in heads]
    m_prev = [m_s[h:h + 1, 0:1] for h in heads]
    qk = [dot(qh[h], kTh[h]) for h in heads]
    qc = [dot(qh[h], c_old[h].astype(BF16)) for h in heads]
    qn = [jnp.sum(qh[h].astype(F32) * n_row[h], axis=1, keepdims=True) for h in heads]

    li_r = [gates[h:h + 1, :] for h in heads]
    lf_r = [gates[ML_HEADS + h:ML_HEADS + h + 1, :] for h in heads]
    b_r = [cum[ML_HEADS + h:ML_HEADS + h + 1, :] for h in heads]
    b_last = [b[:, chunk - 1:chunk] for b in b_r]
    b_c = [jnp.sum(jnp.where(causal, lf, 0.0), axis=1, keepdims=True) for lf in lf_r]
    d_mat = [jnp.where(causal, b_c[h] - b_r[h] + li_r[h], -jnp.inf) for h in heads]
    a = [b_c[h] + m_prev[h] for h in heads]
    mt = [jnp.maximum(a[h], jnp.max(d_mat[h], axis=1, keepdims=True)) for h in heads]
    s = [qk[h] * jnp.exp(d_mat[h] - mt[h]) for h in heads]
    sv = [dot(s[h].astype(BF16), vh[h]) for h in heads]

    g_r = [b_last[h] - b_r[h] + li_r[h] for h in heads]
    m_new = [jnp.maximum(b_last[h] + m_prev[h], jnp.max(g_r[h], axis=1, keepdims=True))
             for h in heads]
    decay = [jnp.exp(b_last[h] + m_prev[h] - m_new[h]) for h in heads]
    wk = [jnp.exp(g_r[h] - m_new[h]) for h in heads]
    kv = [dot((kTh[h].astype(F32) * wk[h]).astype(BF16), vh[h]) for h in heads]
    nk = [lax.dot_general(wk[h].astype(BF16), kTh[h], nt, preferred_element_type=F32)
          for h in heads]

    for h in heads:
        w_inter = jnp.exp(a[h] - mt[h])
        num = w_inter * qc[h] + sv[h]
        den = w_inter * qn[h] + jnp.sum(s[h], axis=1, keepdims=True)
        hid = num / jnp.maximum(jnp.abs(den), jnp.exp(-mt[h]))
        hn = _rms_rows(hid, gain_ref[h:h + 1, :])
        og = jax.nn.sigmoid(o_ref[0, :, h * ML_DV:(h + 1) * ML_DV])
        act_ref[0, :, h * ML_DV:(h + 1) * ML_DV] = (hn * og).astype(BF16)
    for h in heads:
        c_s[h] = decay[h] * c_old[h] + kv[h]
        n_s[h:h + 1, :] = decay[h] * n_row[h] + nk[h]
        m_s[h:h + 1, :] = jnp.broadcast_to(m_new[h], (1, V7X_LANES))

    @pl.when(j == pl.num_programs(1) - 1)
    def _():
        c_out[0] = c_s[...]
        n_out[0] = n_s[0:ML_HEADS, :]
        lane_m = lax.broadcasted_iota(jnp.int32, (1, V7X_LANES), 1)
        m_row = jnp.zeros((1, V7X_LANES), F32)
        for h in range(ML_HEADS):
            m_row = jnp.where(lane_m == h, m_s[h:h + 1, :], m_row)
        m_out[0] = m_row


def _ml_recur_call(q, kT, v, o, gates, out_gain, batch, seq):
    chunk = _row_tile(seq, ML_CHUNK)
    n_steps = seq // chunk
    ng = 2 * ML_HEADS
    q3 = q.reshape(batch, seq, ML_QK_W)
    v3 = v.reshape(batch, seq, ML_V_W)
    o3 = o.reshape(batch, seq, ML_V_W)
    tok = lambda w: pl.BlockSpec((1, chunk, w), lambda b, j: (b, j, 0))
    col = lambda r: pl.BlockSpec((1, r, chunk), lambda b, j: (b, 0, j))
    per_b = lambda *s: pl.BlockSpec((1,) + s, lambda b, j: (b,) + (0,) * len(s))
    act, c_new, n_new, m_new = pl.pallas_call(
        functools.partial(_ml_recur_kernel, chunk=chunk),
        out_shape=(jax.ShapeDtypeStruct((batch, seq, ML_V_W), BF16),
                   jax.ShapeDtypeStruct((batch, ML_HEADS, ML_DK, ML_DV), F32),
                   jax.ShapeDtypeStruct((batch, ML_HEADS, ML_DK), F32),
                   jax.ShapeDtypeStruct((batch, 1, V7X_LANES), F32)),
        grid=(batch, n_steps),
        in_specs=[tok(ML_QK_W), col(ML_QK_W), tok(ML_V_W), tok(ML_V_W), col(ng),
                  pl.BlockSpec((ML_HEADS, ML_DV), lambda b, j: (0, 0))],
        out_specs=(tok(ML_V_W), per_b(ML_HEADS, ML_DK, ML_DV), per_b(ML_HEADS, ML_DK),
                   per_b(1, V7X_LANES)),
        scratch_shapes=[pltpu.VMEM((ML_HEADS, ML_DK, ML_DV), F32),
                        pltpu.VMEM((8, ML_DK), F32),
                        pltpu.VMEM((8, V7X_LANES), F32)],
        compiler_params=_params(("parallel", "arbitrary")),
        name="ml_recur",
    )(q3, kT, v3, o3, gates, out_gain.astype(F32))
    return act.reshape(batch * seq, ML_V_W), c_new, n_new, m_new[:, 0, :ML_HEADS]


def _ml_proj_decode_kernel(x_ref, g_ref, w_ref, wqkT_ref, wg_ref, gb_ref, z_ref, zT_ref, gates_ref):
    h = _rms_rows(x_ref[...], g_ref[...]).astype(BF16)
    z = jnp.dot(h, w_ref[...], preferred_element_type=F32)
    col = lax.broadcasted_iota(jnp.int32, z.shape, 1)
    is_k = (col >= ML_QK_W) & (col < 2 * ML_QK_W)
    z_ref[...] = jnp.where(is_k, z * (ML_DK ** -0.5), z)
    nt = (((1,), (1,)), ((), ()))
    zT = lax.dot_general(wqkT_ref[...], h, nt, preferred_element_type=F32)
    rowi = lax.broadcasted_iota(jnp.int32, zT.shape, 0)
    zT_ref[...] = jnp.where(rowi >= ML_QK_W, zT * (ML_DK ** -0.5), zT)
    g = jnp.dot(h, wg_ref[...], preferred_element_type=F32) + gb_ref[...]
    lane = lax.broadcasted_iota(jnp.int32, g.shape, 1)
    gates_ref[...] = _activate_gates(g, lane >= ML_HEADS)


def _ml_proj_decode_call(x, gain, w_in, gate_bias):
    rows, d = x.shape
    ng = 2 * ML_HEADS
    w_main = w_in[:, :2 * ML_QK_W + 2 * ML_V_W].astype(BF16)
    wqkT = w_in[:, :2 * ML_QK_W].T.astype(BF16)
    wg = jnp.pad(w_in[:, 2 * ML_QK_W + 2 * ML_V_W:], ((0, 0), (0, V7X_LANES - ng))).astype(BF16)
    gb = jnp.pad(gate_bias.astype(F32), (0, V7X_LANES - ng)).reshape(1, V7X_LANES)
    full = lambda a: _const_spec(a.shape)
    args = (x, gain.reshape(1, d), w_main, wqkT, wg, gb)
    out_shape = (jax.ShapeDtypeStruct((rows, w_main.shape[1]), F32),
                 jax.ShapeDtypeStruct((2 * ML_QK_W, rows), F32),
                 jax.ShapeDtypeStruct((rows, V7X_LANES), F32))
    return pl.pallas_call(
        _ml_proj_decode_kernel,
        out_shape=out_shape,
        grid=(1,),
        in_specs=[full(a) for a in args],
        out_specs=tuple(pl.BlockSpec(s.shape, lambda i: (0, 0)) for s in out_shape),
        compiler_params=_params(("arbitrary",)),
        name="ml_proj_decode",
    )(*args)


def _ml_decode_kernel(z_ref, zT_ref, gates_ref, c_ref, n_ref, m_ref, gain_ref,
                      act_ref, c_out, n_out, m_out, *, seqs):
    for r in range(seqs):
        _ml_decode_one(pl.program_id(0) * seqs + r, r, z_ref, zT_ref, gates_ref, c_ref, n_ref,
                       m_ref, gain_ref, act_ref, c_out, n_out, m_out)


def _ml_decode_one(b, r, z_ref, zT_ref, gates_ref, c_ref, n_ref, m_ref, gain_ref,
                   act_ref, c_out, n_out, m_out):
    z = z_ref[pl.ds(b, 1), :]
    g = gates_ref[pl.ds(b, 1), :]
    m_in = m_ref[pl.ds(b, 1), :]
    zT = zT_ref[...]
    pick = lax.broadcasted_iota(jnp.int32, zT.shape, 1) == b
    z_col = jnp.sum(jnp.where(pick, zT, 0.0), axis=1, keepdims=True)
    lane_m = lax.broadcasted_iota(jnp.int32, (1, V7X_LANES), 1)
    m_row = jnp.zeros((1, V7X_LANES), F32)

    for h in range(ML_HEADS):
        li = g[:, h:h + 1]
        lf = g[:, ML_HEADS + h:ML_HEADS + h + 1]
        m_prev = m_in[:, h:h + 1]
        q_row = z[:, h * ML_DK:(h + 1) * ML_DK]
        k_row = z[:, ML_QK_W + h * ML_DK:ML_QK_W + (h + 1) * ML_DK]
        v_row = z[:, 2 * ML_QK_W + h * ML_DV:2 * ML_QK_W + (h + 1) * ML_DV]
        o_row = z[:, 2 * ML_QK_W + ML_V_W + h * ML_DV:2 * ML_QK_W + ML_V_W + (h + 1) * ML_DV]
        q_col = z_col[h * ML_DK:(h + 1) * ML_DK, :]
        k_col = z_col[ML_QK_W + h * ML_DK:ML_QK_W + (h + 1) * ML_DK, :]
        c_old = c_ref[r, h]
        n_row = n_ref[r, h:h + 1, :]

        a = lf + m_prev
        mt = jnp.maximum(a, li)
        w_inter = jnp.exp(a - mt)
        s = jnp.sum(q_row * k_row, axis=1, keepdims=True) * jnp.exp(li - mt)
        num = w_inter * jnp.sum(q_col * c_old, axis=0, keepdims=True) + s * v_row
        den = w_inter * jnp.sum(q_row * n_row, axis=1, keepdims=True) + s
        hid = num / jnp.maximum(jnp.abs(den), jnp.exp(-mt))
        hn = _rms_rows(hid, gain_ref[h:h + 1, :])
        act_ref[r, :, h * ML_DV:(h + 1) * ML_DV] = (hn * jax.nn.sigmoid(o_row)).astype(BF16)

        m_new = jnp.maximum(a, li)
        decay = jnp.exp(a - m_new)
        wk = jnp.exp(li - m_new)
        c_out[r, h] = decay * c_old + (wk * k_col) * v_row
        n_out[r, h:h + 1, :] = decay * n_row + wk * k_row
        m_row = jnp.where(lane_m == h, m_new, m_row)
    m_out[r] = m_row


def _ml_decode_call(z, zT, gates, state_c, state_n, state_m, out_gain):
    rows = z.shape[0]
    seqs = _row_tile(rows, ML_DECODE_SEQS_PER_STEP)
    full = lambda a: _const_spec(a.shape)
    per_b = lambda *s: pl.BlockSpec((seqs,) + s, lambda b: (b,) + (0,) * len(s))
    act, c_new, n_new, m_new = pl.pallas_call(
        functools.partial(_ml_decode_kernel, seqs=seqs),
        out_shape=(jax.ShapeDtypeStruct((rows, 1, ML_V_W), BF16),
                   jax.ShapeDtypeStruct(state_c.shape, F32),
                   jax.ShapeDtypeStruct(state_n.shape, F32),
                   jax.ShapeDtypeStruct((rows, 1, V7X_LANES), F32)),
        grid=(rows // seqs,),
        in_specs=[full(z), full(zT), full(gates), per_b(ML_HEADS, ML_DK, ML_DV),
                  per_b(ML_HEADS, ML_DK), full(state_m), full(out_gain)],
        out_specs=(per_b(1, ML_V_W), per_b(ML_HEADS, ML_DK, ML_DV), per_b(ML_HEADS, ML_DK),
                   per_b(1, V7X_LANES)),
        compiler_params=_params(("arbitrary",)),
        name="ml_decode",
    )(z, zT, gates, state_c, state_n, state_m, out_gain.astype(F32))
    return act.reshape(rows, ML_V_W), c_new, n_new, m_new[:, 0, :ML_HEADS]


def _subhead_mean_square(t, bsum, split):
    t2 = t * t
    hi = t2.astype(BF16)
    lo = (t2 - hi.astype(F32)).astype(BF16) if split else None
    parts = []
    for j in range(t.shape[1] // V7X_MXU_DIM):
        sl = slice(j * V7X_MXU_DIM, (j + 1) * V7X_MXU_DIM)
        part = jnp.dot(hi[:, sl], bsum, preferred_element_type=F32)
        if split:
            part = part + jnp.dot(lo[:, sl], bsum, preferred_element_type=F32)
        parts.append(part)
    return jnp.concatenate(parts, axis=1)


def _rope(t, cos, sin_signed):
    lane = lax.broadcasted_iota(jnp.int32, (t.shape[0], V7X_LANES), 1)
    first = (lane % DA_HEAD_DIM) < ROT_HALF
    parts = []
    for j in range(t.shape[1] // V7X_LANES):
        tj = t[:, j * V7X_LANES:(j + 1) * V7X_LANES]
        partner = jnp.where(first, pltpu.roll(tj, V7X_LANES - ROT_HALF, axis=1),
                            pltpu.roll(tj, ROT_HALF, axis=1))
        parts.append(tj * cos + partner * sin_signed)
    return jnp.concatenate(parts, axis=1)


def _da_proj_kernel(*refs, for_prompt):
    if for_prompt:
        (x_ref, g_ref, w_ref, qg_ref, kg_ref, bsum_ref, cos_ref, sin_ref,
         wkvT_ref, kgc_ref, cosT_ref, sinT_ref, q_ref, v_ref, kb_ref, kT_ref, vT_ref) = refs
    else:
        (x_ref, g_ref, w_ref, qg_ref, kg_ref, bsum_ref, cos_ref, sin_ref,
         q_ref, k_ref, v_ref) = refs
    h = _rms_rows(x_ref[...], g_ref[...]).astype(BF16)
    z = jnp.dot(h, w_ref[...], preferred_element_type=F32)
    bsum = bsum_ref[...]
    cos = cos_ref[...]
    sin = sin_ref[...]
    split = not for_prompt
    q = z[:, :DA_W]
    k = z[:, DA_W:2 * DA_W]
    q = _rope(q * lax.rsqrt(_subhead_mean_square(q, bsum, split) + EPS) * qg_ref[...], cos, sin)
    k = _rope(k * lax.rsqrt(_subhead_mean_square(k, bsum, split) + EPS) * kg_ref[...], cos, sin)
    q_ref[...] = (q * SCORE_SCALE_LOG2).astype(BF16)
    v_ref[...] = z[:, 2 * DA_W:]
    if not for_prompt:
        k_ref[...] = k
        return
    kb_ref[...] = k.astype(BF16)
    nt = (((1,), (1,)), ((), ()))
    kvT = lax.dot_general(wkvT_ref[...], h, nt, preferred_element_type=F32)
    vT_ref[0] = kvT[DA_W:].astype(BF16)
    tm = kvT.shape[1]
    kT = kvT[:DA_W].reshape(DA_SUB, DA_HEAD_DIM, tm)
    ms = jnp.mean(kT * kT, axis=1, keepdims=True)
    kT = kT * lax.rsqrt(ms + EPS) * kgc_ref[...]
    x1 = kT[:, :ROT_HALF]
    x2 = kT[:, ROT_HALF:ROT_DIM]
    cosT = cosT_ref[...]
    sinT = sinT_ref[...]
    kT = jnp.concatenate([x1 * cosT - x2 * sinT, x2 * cosT + x1 * sinT, kT[:, ROT_DIM:]], axis=1)
    kT_ref[0] = kT.reshape(DA_W, tm)


def _rope_tables(pos):
    inv_freq = jnp.power(ROPE_THETA, -jnp.arange(0, ROT_DIM, 2, dtype=F32) / ROT_DIM)
    ang = pos.astype(F32)[:, None] * inv_freq[None, :]
    cos, sin = jnp.cos(ang), jnp.sin(ang)
    ones = jnp.ones((pos.shape[0], DA_HEAD_DIM - ROT_DIM), F32)
    cos64 = jnp.concatenate([cos, cos, ones], axis=1)
    sin64 = jnp.concatenate([-sin, sin, 0.0 * ones], axis=1)
    return jnp.tile(cos64, (1, 2)), jnp.tile(sin64, (1, 2)), cos.T, sin.T


def _da_proj_call(x, gain, w_qkv, q_gain, k_gain, pos, seq, *, for_prompt, row_target=512):
    rows, d = x.shape
    tm = _row_tile(seq, row_target)
    per_seq = seq // tm
    w = w_qkv.astype(BF16)
    cos, sin, cosT, sinT = _rope_tables(pos)
    blk = jnp.arange(V7X_MXU_DIM) // DA_HEAD_DIM
    bsum = jnp.where(blk[:, None] == blk[None, :], 1.0 / DA_HEAD_DIM, 0.0).astype(BF16)
    qg = jnp.tile(q_gain.astype(F32), DA_SUB).reshape(1, DA_W)
    kg = jnp.tile(k_gain.astype(F32), DA_SUB).reshape(1, DA_W)
    row = lambda wd: pl.BlockSpec((tm, wd), lambda i: (i, 0))
    table = pl.BlockSpec((tm, V7X_LANES), lambda i: (i % per_seq, 0))
    n_v = w.shape[1] - 2 * DA_W
    in_specs = [row(d), _const_spec((1, d)), _const_spec(w.shape), _const_spec((1, DA_W)),
                _const_spec((1, DA_W)), _const_spec(bsum.shape), table, table]
    args = [x, gain.reshape(1, d), w, qg, kg, bsum, cos, sin]
    if for_prompt:
        assert n_v == DA_W
        wkvT = w_qkv[:, DA_W:].T.astype(BF16)
        kgc = k_gain.astype(F32).reshape(1, DA_HEAD_DIM, 1)
        tableT = pl.BlockSpec((ROT_HALF, tm), lambda i: (0, i % per_seq))
        by_seq = pl.BlockSpec((1, DA_W, tm), lambda i: (i // per_seq, 0, i % per_seq))
        in_specs += [_const_spec(wkvT.shape), _const_spec(kgc.shape), tableT, tableT]
        args += [wkvT, kgc, cosT, sinT]
        out_shape = [jax.ShapeDtypeStruct((rows, DA_W), BF16), jax.ShapeDtypeStruct((rows, n_v), F32),
                     jax.ShapeDtypeStruct((rows, DA_W), BF16),
                     jax.ShapeDtypeStruct((rows // seq, DA_W, seq), F32),
                     jax.ShapeDtypeStruct((rows // seq, n_v, seq), BF16)]
        out_specs = [row(DA_W), row(n_v), row(DA_W), by_seq, by_seq]
    else:
        out_shape = [jax.ShapeDtypeStruct((rows, DA_W), BF16),
                     jax.ShapeDtypeStruct((rows, DA_W), F32),
                     jax.ShapeDtypeStruct((rows, n_v), F32)]
        out_specs = [row(DA_W), row(DA_W), row(n_v)]
    return pl.pallas_call(
        functools.partial(_da_proj_kernel, for_prompt=for_prompt),
        out_shape=tuple(out_shape),
        grid=(rows // tm,),
        in_specs=in_specs,
        out_specs=tuple(out_specs),
        compiler_params=_params(("parallel",)),
        name="da_proj_prompt" if for_prompt else "da_proj_decode",
    )(*args)


def _lambda_value(lam_ref, lam_init):
    lp = lam_ref[...]
    s1 = jnp.sum(lp[0:1, :] * lp[1:2, :], axis=1, keepdims=True)
    s2 = jnp.sum(lp[2:3, :] * lp[3:4, :], axis=1, keepdims=True)
    return jnp.exp(s1) - jnp.exp(s2) + lam_init


def _da_attn_kernel(q_ref, k_ref, vT_ref, lam_ref, subln_ref, o_ref, q2_s, m_s, l_s, acc_s,
                    *, tq, tk, lam_init):
    i = pl.program_id(1)
    nt = (((1,), (1,)), ((), ()))
    lane = lax.broadcasted_iota(jnp.int32, (tq, V7X_LANES), 1)
    for h in range(DA_HEADS):
        qp = q_ref[0, :, h * DA_VDIM:(h + 1) * DA_VDIM]
        zero = jnp.zeros_like(qp)
        q2_s[h, 0:tq, :] = jnp.where(lane < DA_HEAD_DIM, qp, zero)
        q2_s[h, tq:2 * tq, :] = jnp.where(lane >= DA_HEAD_DIM, qp, zero)
    m_s[...] = jnp.full(m_s.shape, -jnp.inf, F32)
    l_s[...] = jnp.zeros_like(l_s)
    acc_s[...] = jnp.zeros_like(acc_s)

    key_l = lax.broadcasted_iota(jnp.int32, (tk, 2 * tq), 0)
    qry_g = i * tq + lax.broadcasted_iota(jnp.int32, (tk, 2 * tq), 1) % tq

    def block(j, masked):
        start = pl.multiple_of(j * tk, tk)

        def scores(h):
            kb = k_ref[0, pl.ds(start, tk), h * DA_VDIM:(h + 1) * DA_VDIM]
            return lax.dot_general(kb, q2_s[h], nt, preferred_element_type=F32)

        ahead = [scores(h) for h in range(ATTN_HEADS_AHEAD)]
        for h in range(DA_HEADS):
            hs = slice(h * DA_VDIM, (h + 1) * DA_VDIM)
            s = ahead.pop(0)
            if h + ATTN_HEADS_AHEAD < DA_HEADS:
                ahead.append(scores(h + ATTN_HEADS_AHEAD))
            if masked:
                s = jnp.where(j * tk + key_l <= qry_g, s, -jnp.inf)
            m_old = m_s[h]
            m_new = jnp.maximum(m_old, jnp.max(s, axis=0, keepdims=True))
            alpha = jnp.exp2(m_old - m_new)
            p = jnp.exp2(s - m_new)
            l_s[h] = alpha * l_s[h] + jnp.sum(p, axis=0, keepdims=True)
            vTb = vT_ref[0, hs, pl.ds(start, tk)]
            acc_s[h] = alpha * acc_s[h] + jnp.dot(vTb, p.astype(BF16),
                                                  preferred_element_type=F32)
            m_s[h] = m_new

    n_full = (i * tq) // tk

    def full_block(j, carry):
        block(j, False)
        return carry

    lax.fori_loop(0, n_full, full_block, 0)
    block(n_full, True)

    lam = _lambda_value(lam_ref, lam_init)
    for h in range(DA_HEADS):
        oT = acc_s[h] / l_s[h]
        oT = oT[:, :tq] - lam * oT[:, tq:]
        ms = jnp.mean(oT * oT, axis=0, keepdims=True)
        oT = oT * lax.rsqrt(ms + EPS) * (subln_ref[...] * (1.0 - lam_init))
        o_ref[0, :, h * DA_VDIM:(h + 1) * DA_VDIM] = oT.T.astype(BF16)


def _da_attn_call(q, kb, vT, lam_p, subln, batch, seq, lam_init):
    tk = _row_tile(seq, ATTN_KV_BLOCK)
    tq = min(_row_tile(seq, ATTN_Q_BLOCK), tk)
    assert tk % tq == 0
    w = q.shape[1]
    q3, k3 = (t.reshape(batch, seq, t.shape[1]) for t in (q, kb))
    out = pl.pallas_call(
        functools.partial(_da_attn_kernel, tq=tq, tk=tk, lam_init=lam_init),
        out_shape=jax.ShapeDtypeStruct((batch, seq, vT.shape[1]), BF16),
        grid=(batch, seq // tq),
        in_specs=[pl.BlockSpec((1, tq, w), lambda b, i: (b, i, 0)),
                  pl.BlockSpec((1, seq, w), lambda b, i: (b, 0, 0)),
                  pl.BlockSpec((1, vT.shape[1], seq), lambda b, i: (b, 0, 0)),
                  pl.BlockSpec(lam_p.shape, lambda b, i: (0, 0)),
                  pl.BlockSpec((DA_VDIM, 1), lambda b, i: (0, 0))],
        out_specs=pl.BlockSpec((1, tq, vT.shape[1]), lambda b, i: (b, i, 0)),
        scratch_shapes=[pltpu.VMEM((DA_HEADS, 2 * tq, DA_VDIM), BF16),
                        pltpu.VMEM((DA_HEADS, 1, 2 * tq), F32),
                        pltpu.VMEM((DA_HEADS, 1, 2 * tq), F32),
                        pltpu.VMEM((DA_HEADS, DA_VDIM, 2 * tq), F32)],
        compiler_params=_params(("parallel", "arbitrary")),
        name="da_attn_prompt",
    )(q3, k3, vT, lam_p.astype(F32), subln.reshape(DA_VDIM, 1).astype(F32))
    return out.reshape(batch * seq, vT.shape[1])


def _da_decode_kernel(pt_ref, q_ref, kn_ref, vn_ref, lam_ref, subln_ref, expand_ref, *rest,
                      pages, lam_init):
    k_refs = rest[:pages]
    v_refs = rest[pages:2 * pages]
    o_ref, qbd_s, m_s, l_s, acc_s = rest[2 * pages:]
    j = pl.program_id(1)
    flat = (DA_SUB * DA_HEAD_DIM, PAGE_SIZE)

    @pl.when(j == 0)
    def _():
        q = q_ref[0]
        sub = lax.broadcasted_iota(jnp.int32, (DA_SUB, DA_W), 0)
        col = lax.broadcasted_iota(jnp.int32, (DA_SUB, DA_W), 1)
        qb = jnp.broadcast_to(q.astype(F32), (DA_SUB, DA_W))
        qbd_s[...] = jnp.where(col // DA_HEAD_DIM == sub, qb, 0.0).astype(BF16)
        m_s[...] = jnp.full(m_s.shape, -jnp.inf, F32)
        l_s[...] = jnp.zeros_like(l_s)
        acc_s[...] = jnp.zeros_like(acc_s)

    qbd = qbd_s[...]
    s = jnp.concatenate(
        [jnp.dot(qbd, k_refs[i][...].reshape(flat).astype(BF16), preferred_element_type=F32)
         for i in range(pages)], axis=1)
    m_old = m_s[...]
    m_new = jnp.maximum(m_old, jnp.max(s, axis=1, keepdims=True))
    alpha = jnp.exp2(m_old - m_new)
    p = jnp.exp2(s - m_new)
    l_s[...] = alpha * l_s[...] + jnp.sum(p, axis=1, keepdims=True)
    p_rows = jnp.concatenate([p[:, i * PAGE_SIZE:(i + 1) * PAGE_SIZE] for i in range(pages)],
                             axis=0).astype(BF16)
    w = jnp.dot(p_rows, expand_ref[...], preferred_element_type=F32)
    sub = lax.broadcasted_iota(jnp.int32, w.shape, 0) % DA_SUB
    head = lax.broadcasted_iota(jnp.int32, w.shape, 1) % DA_HEADS
    w = jnp.where(head == sub // 2, w, 0.0).astype(BF16)
    pv = jnp.zeros(acc_s.shape, F32)
    for i in range(pages):
        pv += jnp.dot(w[i * DA_SUB:(i + 1) * DA_SUB], v_refs[i][...].reshape(flat).astype(BF16),
                      preferred_element_type=F32)
    acc_s[...] = alpha * acc_s[...] + pv
    m_s[...] = m_new

    @pl.when(j == pl.num_programs(1) - 1)
    def _():
        lam = _lambda_value(lam_ref, lam_init)
        m_p = m_s[...]
        s_self = jnp.sum(qbd.astype(F32) * kn_ref[0], axis=1, keepdims=True)
        m_f = jnp.maximum(m_p, s_self)
        a_f = jnp.exp2(m_p - m_f)
        p_self = jnp.exp2(s_self - m_f)
        l_f = a_f * l_s[...] + p_self
        past = a_f * acc_s[...]
        v_self = vn_ref[0]
        for h in range(DA_HEADS):
            o2 = ((past[2 * h:2 * h + 2] + p_self[2 * h:2 * h + 2] * v_self[h:h + 1])
                  / l_f[2 * h:2 * h + 2])
            oh = o2[0:1] - lam * o2[1:2]
            oh = _rms_rows(oh, subln_ref[...]) * (1.0 - lam_init)
            o_ref[0, h:h + 1, :] = oh.astype(BF16)


def _da_decode_call(q, k_new, v_new, cache_k, cache_v, layer, page_table, lam_p, subln, lam_init):
    rows = q.shape[0]
    n_pages = page_table.shape[1]
    pages = DECODE_PAGES_PER_STEP
    while n_pages % pages:
        pages //= 2
    ck = jnp.transpose(cache_k, (0, 1, 3, 4, 2))
    per_b = lambda a: pl.BlockSpec((1,) + a.shape[1:], lambda b, j, pt: (b, 0, 0))
    page_index = lambda i: (lambda b, j, pt: (layer, pt[b, j * pages + i], 0, 0, 0))
    k_spec = lambda i: pl.BlockSpec((None, None, DA_SUB, DA_HEAD_DIM, PAGE_SIZE), page_index(i))
    v_spec = lambda i: pl.BlockSpec((None, None, PAGE_SIZE, DA_HEADS, DA_VDIM), page_index(i))
    token = jnp.arange(PAGE_SIZE * DA_HEADS) // DA_HEADS
    expand = (token[None, :] == jnp.arange(PAGE_SIZE)[:, None]).astype(BF16)

    q3 = q.reshape(rows, 1, DA_W)
    kn3 = k_new.reshape(rows, 1, DA_W)
    vn3 = v_new.reshape(rows, DA_HEADS, DA_VDIM)
    grid_spec = pltpu.PrefetchScalarGridSpec(
        num_scalar_prefetch=1,
        grid=(rows, n_pages // pages),
        in_specs=[per_b(q3), per_b(kn3), per_b(vn3),
                  pl.BlockSpec(lam_p.shape, lambda b, j, pt: (0, 0)),
                  pl.BlockSpec((1, DA_VDIM), lambda b, j, pt: (0, 0)),
                  pl.BlockSpec(expand.shape, lambda b, j, pt: (0, 0))]
        + [k_spec(i) for i in range(pages)] + [v_spec(i) for i in range(pages)],
        out_specs=pl.BlockSpec((1, DA_HEADS, DA_VDIM), lambda b, j, pt: (b, 0, 0)),
        scratch_shapes=[pltpu.VMEM((DA_SUB, DA_W), BF16), pltpu.VMEM((DA_SUB, 1), F32),
                        pltpu.VMEM((DA_SUB, 1), F32), pltpu.VMEM((DA_SUB, DA_VDIM), F32)],
    )
    out = pl.pallas_call(
        functools.partial(_da_decode_kernel, pages=pages, lam_init=lam_init),
        out_shape=jax.ShapeDtypeStruct((rows, DA_HEADS, DA_VDIM), BF16),
        grid_spec=grid_spec,
        compiler_params=_params(("parallel", "arbitrary")),
        name="da_attn_decode",
    )(page_table, q3, kn3, vn3, lam_p.astype(F32), subln.reshape(1, DA_VDIM).astype(F32), expand,
      *([ck] * pages), *([cache_v] * pages))
    return out.reshape(rows, DA_HEADS * DA_VDIM)


def kernel(x_prompt, x_sample, state_C, state_n, state_m, cache_k, cache_v, page_table,
           ffn1_norm, ffn1_w_gu, ffn1_w_down, mix_norm, ffn2_norm, ffn2_w_gu, ffn2_w_down,
           ml_w_in, ml_gate_bias, ml_out_norm, ml_w_out,
           da_w_qkv, da_q_norm, da_k_norm, da_lambda, da_subln, da_w_out):
    depth = ffn1_norm.shape[0]
    bp, lp, d = x_prompt.shape
    bs, ls, _ = x_sample.shape
    assert ls == 1
    past_len = page_table.shape[1] * PAGE_SIZE
    pos_p = jnp.arange(lp, dtype=jnp.int32)
    pos_s = past_len + jnp.arange(ls, dtype=jnp.int32)

    xp = x_prompt.reshape(bp * lp, d)
    xs = x_sample.reshape(bs * ls, d)
    ml_p, ml_s, kv_p, kv_s = [], [], [], []
    ffn1 = (ffn1_norm.astype(F32).reshape(depth, 1, d), ffn1_w_gu.astype(BF16),
            ffn1_w_down.astype(BF16))
    ffn2 = (ffn2_norm.astype(F32).reshape(depth, 1, d), ffn2_w_gu.astype(BF16),
            ffn2_w_down.astype(BF16))
    for i in range(depth):
        j = i // N_MIXERS
        xp, xs = _ffn_call(xp, xs, *ffn1, i)
        if i % N_MIXERS == 0:
            q, kT, v, o, gates = _ml_proj_prompt_call(xp, mix_norm[i], ml_w_in[j], ml_gate_bias[j],
                                                      bp, lp)
            act_p, c_p, n_p, m_p = _ml_recur_call(q, kT, v, o, gates, ml_out_norm[j], bp, lp)
            ml_p.append((c_p, n_p, m_p))
            z, zT, gts = _ml_proj_decode_call(xs, mix_norm[i], ml_w_in[j], ml_gate_bias[j])
            act_s, c_s, n_s, m_s = _ml_decode_call(z, zT, gts, state_C[j], state_n[j], state_m[j],
                                                   ml_out_norm[j])
            ml_s.append((c_s, n_s, m_s))
            w_out = ml_w_out[j].astype(BF16)
        else:
            lam_init = _lambda_init(i)
            qp, vp, kpb, kpT, vpT = _da_proj_call(xp, mix_norm[i], da_w_qkv[j], da_q_norm[j],
                                                  da_k_norm[j], pos_p, lp, for_prompt=True)
            act_p = _da_attn_call(qp, kpb, vpT, da_lambda[j], da_subln[j], bp, lp, lam_init)
            kp = jnp.transpose(kpT.reshape(bp, DA_SUB, DA_HEAD_DIM, lp), (0, 3, 1, 2))
            kv_p.append((kp, vp.reshape(bp, lp, DA_HEADS, DA_VDIM)))
            qs, ks, vs = _da_proj_call(xs, mix_norm[i], da_w_qkv[j], da_q_norm[j], da_k_norm[j],
                                       jnp.repeat(pos_s, bs), bs, for_prompt=False)
            act_s = _da_decode_call(qs, ks, vs, cache_k, cache_v, j, page_table,
                                    da_lambda[j], da_subln[j], lam_init)
            kv_s.append((ks.reshape(bs, ls, DA_SUB, DA_HEAD_DIM),
                         vs.reshape(bs, ls, DA_HEADS, DA_VDIM)))
            w_out = da_w_out[j].astype(BF16)
        xp, xs = _ffn_call(xp, xs, *ffn2, i, pre=(act_p, act_s, w_out))

    stack = lambda items, k: jnp.stack([it[k] for it in items])
    return (xp.reshape(bp, lp, d), xs.reshape(bs, ls, d),
            stack(ml_p, 0), stack(ml_p, 1), stack(ml_p, 2),
            stack(ml_s, 0), stack(ml_s, 1), stack(ml_s, 2),
            stack(kv_p, 0), stack(kv_p, 1), stack(kv_s, 0), stack(kv_s, 1))
```

```python
import functools
import math

import jax
import jax.numpy as jnp
from jax import lax
from jax.experimental import pallas as pl
from jax.experimental.pallas import tpu as pltpu

F32 = jnp.float32
BF16 = jnp.bfloat16

EPS = 1e-6
ML_HEADS = 4
ML_DK = 128
ML_DV = 256
ML_QK_W = ML_HEADS * ML_DK
ML_V_W = ML_HEADS * ML_DV
GATE_SOFTCAP = 15.0
DA_HEAD_DIM = 64
DA_HEADS = 8
DA_SUB = 2 * DA_HEADS
DA_VDIM = 128
DA_W = DA_SUB * DA_HEAD_DIM
ROPE_THETA = 500000.0
ROT_DIM = DA_HEAD_DIM // 4
ROT_HALF = ROT_DIM // 2
PAGE_SIZE = 128
N_MIXERS = 2
SCORE_SCALE_LOG2 = (DA_HEAD_DIM ** -0.5) * math.log2(math.e)

V7X_LANES = 128
V7X_MXU_DIM = 256
V7X_VMEM_LIMIT_BYTES = 56 * 1024 * 1024

FFN_CHUNK = 256
ML_CHUNK = 128
ATTN_Q_BLOCK = 256
ATTN_KV_BLOCK = 256
ATTN_HEADS_AHEAD = 2
DECODE_PAGES_PER_STEP = 16
ML_DECODE_SEQS_PER_STEP = 4


def _lambda_init(layer):
    return 0.8 - 0.6 * math.exp(-0.3 * layer)


def _const_spec(shape):
    zeros = (0,) * len(shape)
    return pl.BlockSpec(shape, lambda *_: zeros, pipeline_mode=pl.Buffered(1))


def _params(semantics):
    return pltpu.CompilerParams(dimension_semantics=semantics,
                                vmem_limit_bytes=V7X_VMEM_LIMIT_BYTES)


def _row_tile(rows, target):
    t = min(rows, target)
    while rows % t:
        t //= 2
    return t


def _rms_rows(x, gain):
    ms = jnp.mean(x * x, axis=-1, keepdims=True)
    return x * lax.rsqrt(ms + EPS) * gain


def _ffn_kernel(*refs, has_pre, d_ff, tf):
    if has_pre:
        x_ref, a_ref, xs_ref, as_ref, wo_ref, g_ref, wgu_ref, wd_ref, o_ref, os_ref, h_scr = refs
    else:
        x_ref, xs_ref, g_ref, wgu_ref, wd_ref, o_ref, os_ref, h_scr = refs
        a_ref = as_ref = None

    def run(x_in, a_in, out):
        rows = x_in.shape[0]
        x = x_in[...]
        if has_pre:
            x = x + jnp.dot(a_in[...], wo_ref[...], preferred_element_type=F32)
        h_scr[0:rows, :] = _rms_rows(x, g_ref[...]).astype(BF16)
        out[...] = x
        for c in range(d_ff // tf):
            h = h_scr[0:rows, :]
            g = jnp.dot(h, wgu_ref[:, c * tf:(c + 1) * tf], preferred_element_type=F32)
            u = jnp.dot(h, wgu_ref[:, d_ff + c * tf:d_ff + (c + 1) * tf],
                        preferred_element_type=F32)
            act = (0.5 * g * jax.nn.sigmoid(g) * u).astype(BF16)
            out[...] += jnp.dot(act, wd_ref[c * tf:(c + 1) * tf, :], preferred_element_type=F32)

    run(x_ref, a_ref, o_ref)

    @pl.when(pl.program_id(0) == pl.num_programs(0) - 1)
    def _():
        run(xs_ref, as_ref, os_ref)


def _ffn_call(x, xs, gains, w_gu, w_down, layer, pre=None, *, row_target=512):
    rows, d = x.shape
    d_ff = w_down.shape[1]
    tf = FFN_CHUNK
    while d_ff % tf:
        tf //= 2
    tm = _row_tile(rows, row_target)
    assert xs.shape[0] <= tm
    row_spec = pl.BlockSpec((tm, d), lambda i: (i, 0))
    whole = lambda a: _const_spec(a.shape)
    of_layer = lambda a: pl.BlockSpec((None,) + a.shape[1:], lambda i: (layer, 0, 0),
                                      pipeline_mode=pl.Buffered(1))
    in_specs, args = [row_spec], [x]
    if pre is not None:
        a, a_s, wo = pre
        in_specs += [pl.BlockSpec((tm, a.shape[1]), lambda i: (i, 0)), whole(xs), whole(a_s),
                     whole(wo)]
        args += [a, xs, a_s, wo]
    else:
        in_specs.append(whole(xs))
        args.append(xs)
    in_specs += [of_layer(gains), of_layer(w_gu), of_layer(w_down)]
    args += [gains, w_gu, w_down]
    return pl.pallas_call(
        functools.partial(_ffn_kernel, has_pre=pre is not None, d_ff=d_ff, tf=tf),
        out_shape=(jax.ShapeDtypeStruct((rows, d), F32), jax.ShapeDtypeStruct(xs.shape, F32)),
        grid=(rows // tm,),
        in_specs=in_specs,
        out_specs=(row_spec, pl.BlockSpec(xs.shape, lambda i: (0, 0))),
        scratch_shapes=[pltpu.VMEM((tm, d), BF16)],
        compiler_params=_params(("arbitrary",)),
        name="ffn_pre" if pre is not None else "ffn",
    )(*args)


def _activate_gates(g, is_forget):
    g = GATE_SOFTCAP * jnp.tanh(g / GATE_SOFTCAP)
    log_sig = jnp.minimum(g, 0.0) - jnp.log1p(jnp.exp(-jnp.abs(g)))
    return jnp.where(is_forget, log_sig, g)


def _ml_proj_prompt_kernel(x_ref, g_ref, wqvo_ref, wkT_ref, wgT_ref, gb_ref,
                           q_ref, kT_ref, v_ref, o_ref, gates_ref):
    h = _rms_rows(x_ref[...], g_ref[...]).astype(BF16)
    z = jnp.dot(h, wqvo_ref[...], preferred_element_type=F32)
    q_ref[...] = z[:, :ML_QK_W].astype(BF16)
    v_ref[...] = z[:, ML_QK_W:ML_QK_W + ML_V_W].astype(BF16)
    o_ref[...] = z[:, ML_QK_W + ML_V_W:]
    nt = (((1,), (1,)), ((), ()))
    kT = lax.dot_general(wkT_ref[...], h, nt, preferred_element_type=F32)
    kT_ref[0] = (kT * (ML_DK ** -0.5)).astype(BF16)
    gT = lax.dot_general(wgT_ref[...], h, nt, preferred_element_type=F32) + gb_ref[...]
    row = lax.broadcasted_iota(jnp.int32, gT.shape, 0)
    gates_ref[0] = _activate_gates(gT, row >= ML_HEADS)


def _ml_proj_prompt_call(x, gain, w_in, gate_bias, batch, seq, *, row_target=512):
    rows, d = x.shape
    tm = _row_tile(seq, row_target)
    per_seq = seq // tm
    ng = 2 * ML_HEADS
    wqvo = jnp.concatenate([w_in[:, :ML_QK_W], w_in[:, 2 * ML_QK_W:2 * ML_QK_W + 2 * ML_V_W]],
                           axis=1).astype(BF16)
    wkT = w_in[:, ML_QK_W:2 * ML_QK_W].T.astype(BF16)
    wgT = w_in[:, 2 * ML_QK_W + 2 * ML_V_W:].T.astype(BF16)
    row = lambda w: pl.BlockSpec((tm, w), lambda i: (i, 0))
    by_seq = lambda r: pl.BlockSpec((1, r, tm), lambda i: (i // per_seq, 0, i % per_seq))
    return pl.pallas_call(
        _ml_proj_prompt_kernel,
        out_shape=(jax.ShapeDtypeStruct((rows, ML_QK_W), BF16),
                   jax.ShapeDtypeStruct((batch, ML_QK_W, seq), BF16),
                   jax.ShapeDtypeStruct((rows, ML_V_W), BF16),
                   jax.ShapeDtypeStruct((rows, ML_V_W), F32),
                   jax.ShapeDtypeStruct((batch, ng, seq), F32)),
        grid=(rows // tm,),
        in_specs=[row(d), _const_spec((1, d)), _const_spec(wqvo.shape), _const_spec(wkT.shape),
                  _const_spec(wgT.shape), _const_spec((ng, 1))],
        out_specs=(row(ML_QK_W), by_seq(ML_QK_W), row(ML_V_W), row(ML_V_W), by_seq(ng)),
        compiler_params=_params(("parallel",)),
        name="ml_proj_prompt",
    )(x, gain.reshape(1, d), wqvo, wkT, wgT, gate_bias.reshape(ng, 1).astype(F32))


def _ml_recur_kernel(q_ref, kT_ref, v_ref, o_ref, gates_ref, gain_ref,
                     act_ref, c_out, n_out, m_out, c_s, n_s, m_s, *, chunk):
    j = pl.program_id(1)

    @pl.when(j == 0)
    def _():
        c_s[...] = jnp.zeros_like(c_s)
        n_s[...] = jnp.zeros_like(n_s)
        m_s[...] = jnp.zeros_like(m_s)

    gates = gates_ref[0]
    lane = lax.broadcasted_iota(jnp.int32, gates.shape, 1)
    cum = gates
    step = 1
    while step < chunk:
        cum = cum + jnp.where(lane >= step, pltpu.roll(cum, step, axis=1), 0.0)
        step *= 2
    t_idx = lax.broadcasted_iota(jnp.int32, (chunk, chunk), 0)
    s_idx = lax.broadcasted_iota(jnp.int32, (chunk, chunk), 1)
    causal = s_idx <= t_idx
    nt = (((1,), (1,)), ((), ()))

    heads = range(ML_HEADS)
    dot = functools.partial(jnp.dot, preferred_element_type=F32)
    qh = [q_ref[0, :, h * ML_DK:(h + 1) * ML_DK] for h in heads]
    kTh = [kT_ref[0, h * ML_DK:(h + 1) * ML_DK, :] for h in heads]
    vh = [v_ref[0, :, h * ML_DV:(h + 1) * ML_DV] for h in heads]
    c_old = [c_s[h] for h in heads]
    n_row = [n_s[h:h + 1, :] for h in heads]
    m_prev = [m_s[h:h + 1, 0:1] for h in heads]
    qk = [dot(qh[h], kTh[h]) for h in heads]
    qc = [dot(qh[h], c_old[h].astype(BF16)) for h in heads]
    qn = [jnp.sum(qh[h].astype(F32) * n_row[h], axis=1, keepdims=True) for h in heads]

    li_r = [gates[h:h + 1, :] for h in heads]
    lf_r = [gates[ML_HEADS + h:ML_HEADS + h + 1, :] for h in heads]
    b_r = [cum[ML_HEADS + h:ML_HEADS + h + 1, :] for h in heads]
    b_last = [b[:, chunk - 1:chunk] for b in b_r]
    b_c = [jnp.sum(jnp.where(causal, lf, 0.0), axis=1, keepdims=True) for lf in lf_r]
    d_mat = [jnp.where(causal, b_c[h] - b_r[h] + li_r[h], -jnp.inf) for h in heads]
    a = [b_c[h] + m_prev[h] for h in heads]
    mt = [jnp.maximum(a[h], jnp.max(d_mat[h], axis=1, keepdims=True)) for h in heads]
    s = [qk[h] * jnp.exp(d_mat[h] - mt[h]) for h in heads]
    sv = [dot(s[h].astype(BF16), vh[h]) for h in heads]

    g_r = [b_last[h] - b_r[h] + li_r[h] for h in heads]
    m_new = [jnp.maximum(b_last[h] + m_prev[h], jnp.max(g_r[h], axis=1, keepdims=True))
             for h in heads]
    decay = [jnp.exp(b_last[h] + m_prev[h] - m_new[h]) for h in heads]
    wk = [jnp.exp(g_r[h] - m_new[h]) for h in heads]
    kv = [dot((kTh[h].astype(F32) * wk[h]).astype(BF16), vh[h]) for h in heads]
    nk = [lax.dot_general(wk[h].astype(BF16), kTh[h], nt, preferred_element_type=F32)
          for h in heads]

    for h in heads:
        w_inter = jnp.exp(a[h] - mt[h])
        num = w_inter * qc[h] + sv[h]
        den = w_inter * qn[h] + jnp.sum(s[h], axis=1, keepdims=True)
        hid = num / jnp.maximum(jnp.abs(den), jnp.exp(-mt[h]))
        hn = _rms_rows(hid, gain_ref[h:h + 1, :])
        og = jax.nn.sigmoid(o_ref[0, :, h * ML_DV:(h + 1) * ML_DV])
        act_ref[0, :, h * ML_DV:(h + 1) * ML_DV] = (hn * og).astype(BF16)
    for h in heads:
        c_s[h] = decay[h] * c_old[h] + kv[h]
        n_s[h:h + 1, :] = decay[h] * n_row[h] + nk[h]
        m_s[h:h + 1, :] = jnp.broadcast_to(m_new[h], (1, V7X_LANES))

    @pl.when(j == pl.num_programs(1) - 1)
    def _():
        c_out[0] = c_s[...]
        n_out[0] = n_s[0:ML_HEADS, :]
        lane_m = lax.broadcasted_iota(jnp.int32, (1, V7X_LANES), 1)
        m_row = jnp.zeros((1, V7X_LANES), F32)
        for h in range(ML_HEADS):
            m_row = jnp.where(lane_m == h, m_s[h:h + 1, :], m_row)
        m_out[0] = m_row


def _ml_recur_call(q, kT, v, o, gates, out_gain, batch, seq):
    chunk = _row_tile(seq, ML_CHUNK)
    n_steps = seq // chunk
    ng = 2 * ML_HEADS
    q3 = q.reshape(batch, seq, ML_QK_W)
    v3 = v.reshape(batch, seq, ML_V_W)
    o3 = o.reshape(batch, seq, ML_V_W)
    tok = lambda w: pl.BlockSpec((1, chunk, w), lambda b, j: (b, j, 0))
    col = lambda r: pl.BlockSpec((1, r, chunk), lambda b, j: (b, 0, j))
    per_b = lambda *s: pl.BlockSpec((1,) + s, lambda b, j: (b,) + (0,) * len(s))
    act, c_new, n_new, m_new = pl.pallas_call(
        functools.partial(_ml_recur_kernel, chunk=chunk),
        out_shape=(jax.ShapeDtypeStruct((batch, seq, ML_V_W), BF16),
                   jax.ShapeDtypeStruct((batch, ML_HEADS, ML_DK, ML_DV), F32),
                   jax.ShapeDtypeStruct((batch, ML_HEADS, ML_DK), F32),
                   jax.ShapeDtypeStruct((batch, 1, V7X_LANES), F32)),
        grid=(batch, n_steps),
        in_specs=[tok(ML_QK_W), col(ML_QK_W), tok(ML_V_W), tok(ML_V_W), col(ng),
                  pl.BlockSpec((ML_HEADS, ML_DV), lambda b, j: (0, 0))],
        out_specs=(tok(ML_V_W), per_b(ML_HEADS, ML_DK, ML_DV), per_b(ML_HEADS, ML_DK),
                   per_b(1, V7X_LANES)),
        scratch_shapes=[pltpu.VMEM((ML_HEADS, ML_DK, ML_DV), F32),
                        pltpu.VMEM((8, ML_DK), F32),
                        pltpu.VMEM((8, V7X_LANES), F32)],
        compiler_params=_params(("parallel", "arbitrary")),
        name="ml_recur",
    )(q3, kT, v3, o3, gates, out_gain.astype(F32))
    return act.reshape(batch * seq, ML_V_W), c_new, n_new, m_new[:, 0, :ML_HEADS]


def _ml_proj_decode_kernel(x_ref, g_ref, w_ref, wqkT_ref, wg_ref, gb_ref, z_ref, zT_ref, gates_ref):
    h = _rms_rows(x_ref[...], g_ref[...]).astype(BF16)
    z = jnp.dot(h, w_ref[...], preferred_element_type=F32)
    col = lax.broadcasted_iota(jnp.int32, z.shape, 1)
    is_k = (col >= ML_QK_W) & (col < 2 * ML_QK_W)
    z_ref[...] = jnp.where(is_k, z * (ML_DK ** -0.5), z)
    nt = (((1,), (1,)), ((), ()))
    zT = lax.dot_general(wqkT_ref[...], h, nt, preferred_element_type=F32)
    rowi = lax.broadcasted_iota(jnp.int32, zT.shape, 0)
    zT_ref[...] = jnp.where(rowi >= ML_QK_W, zT * (ML_DK ** -0.5), zT)
    g = jnp.dot(h, wg_ref[...], preferred_element_type=F32) + gb_ref[...]
    lane = lax.broadcasted_iota(jnp.int32, g.shape, 1)
    gates_ref[...] = _activate_gates(g, lane >= ML_HEADS)


def _ml_proj_decode_call(x, gain, w_in, gate_bias):
    rows, d = x.shape
    ng = 2 * ML_HEADS
    w_main = w_in[:, :2 * ML_QK_W + 2 * ML_V_W].astype(BF16)
    wqkT = w_in[:, :2 * ML_QK_W].T.astype(BF16)
    wg = jnp.pad(w_in[:, 2 * ML_QK_W + 2 * ML_V_W:], ((0, 0), (0, V7X_LANES - ng))).astype(BF16)
    gb = jnp.pad(gate_bias.astype(F32), (0, V7X_LANES - ng)).reshape(1, V7X_LANES)
    full = lambda a: _const_spec(a.shape)
    args = (x, gain.reshape(1, d), w_main, wqkT, wg, gb)
    out_shape = (jax.ShapeDtypeStruct((rows, w_main.shape[1]), F32),
                 jax.ShapeDtypeStruct((2 * ML_QK_W, rows), F32),
                 jax.ShapeDtypeStruct((rows, V7X_LANES), F32))
    return pl.pallas_call(
        _ml_proj_decode_kernel,
        out_shape=out_shape,
        grid=(1,),
        in_specs=[full(a) for a in args],
        out_specs=tuple(pl.BlockSpec(s.shape, lambda i: (0, 0)) for s in out_shape),
        compiler_params=_params(("arbitrary",)),
        name="ml_proj_decode",
    )(*args)


def _ml_decode_kernel(z_ref, zT_ref, gates_ref, c_ref, n_ref, m_ref, gain_ref,
                      act_ref, c_out, n_out, m_out, *, seqs):
    for r in range(seqs):
        _ml_decode_one(pl.program_id(0) * seqs + r, r, z_ref, zT_ref, gates_ref, c_ref, n_ref,
                       m_ref, gain_ref, act_ref, c_out, n_out, m_out)


def _ml_decode_one(b, r, z_ref, zT_ref, gates_ref, c_ref, n_ref, m_ref, gain_ref,
                   act_ref, c_out, n_out, m_out):
    z = z_ref[pl.ds(b, 1), :]
    g = gates_ref[pl.ds(b, 1), :]
    m_in = m_ref[pl.ds(b, 1), :]
    zT = zT_ref[...]
    pick = lax.broadcasted_iota(jnp.int32, zT.shape, 1) == b
    z_col = jnp.sum(jnp.where(pick, zT, 0.0), axis=1, keepdims=True)
    lane_m = lax.broadcasted_iota(jnp.int32, (1, V7X_LANES), 1)
    m_row = jnp.zeros((1, V7X_LANES), F32)

    for h in range(ML_HEADS):
        li = g[:, h:h + 1]
        lf = g[:, ML_HEADS + h:ML_HEADS + h + 1]
        m_prev = m_in[:, h:h + 1]
        q_row = z[:, h * ML_DK:(h + 1) * ML_DK]
        k_row = z[:, ML_QK_W + h * ML_DK:ML_QK_W + (h + 1) * ML_DK]
        v_row = z[:, 2 * ML_QK_W + h * ML_DV:2 * ML_QK_W + (h + 1) * ML_DV]
        o_row = z[:, 2 * ML_QK_W + ML_V_W + h * ML_DV:2 * ML_QK_W + ML_V_W + (h + 1) * ML_DV]
        q_col = z_col[h * ML_DK:(h + 1) * ML_DK, :]
        k_col = z_col[ML_QK_W + h * ML_DK:ML_QK_W + (h + 1) * ML_DK, :]
        c_old = c_ref[r, h]
        n_row = n_ref[r, h:h + 1, :]

        a = lf + m_prev
        mt = jnp.maximum(a, li)
        w_inter = jnp.exp(a - mt)
        s = jnp.sum(q_row * k_row, axis=1, keepdims=True) * jnp.exp(li - mt)
        num = w_inter * jnp.sum(q_col * c_old, axis=0, keepdims=True) + s * v_row
        den = w_inter * jnp.sum(q_row * n_row, axis=1, keepdims=True) + s
        hid = num / jnp.maximum(jnp.abs(den), jnp.exp(-mt))
        hn = _rms_rows(hid, gain_ref[h:h + 1, :])
        act_ref[r, :, h * ML_DV:(h + 1) * ML_DV] = (hn * jax.nn.sigmoid(o_row)).astype(BF16)

        m_new = jnp.maximum(a, li)
        decay = jnp.exp(a - m_new)
        wk = jnp.exp(li - m_new)
        c_out[r, h] = decay * c_old + (wk * k_col) * v_row
        n_out[r, h:h + 1, :] = decay * n_row + wk * k_row
        m_row = jnp.where(lane_m == h, m_new, m_row)
    m_out[r] = m_row


def _ml_decode_call(z, zT, gates, state_c, state_n, state_m, out_gain):
    rows = z.shape[0]
    seqs = _row_tile(rows, ML_DECODE_SEQS_PER_STEP)
    full = lambda a: _const_spec(a.shape)
    per_b = lambda *s: pl.BlockSpec((seqs,) + s, lambda b: (b,) + (0,) * len(s))
    act, c_new, n_new, m_new = pl.pallas_call(
        functools.partial(_ml_decode_kernel, seqs=seqs),
        out_shape=(jax.ShapeDtypeStruct((rows, 1, ML_V_W), BF16),
                   jax.ShapeDtypeStruct(state_c.shape, F32),
                   jax.ShapeDtypeStruct(state_n.shape, F32),
                   jax.ShapeDtypeStruct((rows, 1, V7X_LANES), F32)),
        grid=(rows // seqs,),
        in_specs=[full(z), full(zT), full(gates), per_b(ML_HEADS, ML_DK, ML_DV),
                  per_b(ML_HEADS, ML_DK), full(state_m), full(out_gain)],
        out_specs=(per_b(1, ML_V_W), per_b(ML_HEADS, ML_DK, ML_DV), per_b(ML_HEADS, ML_DK),
                   per_b(1, V7X_LANES)),
        compiler_params=_params(("arbitrary",)),
        name="ml_decode",
    )(z, zT, gates, state_c, state_n, state_m, out_gain.astype(F32))
    return act.reshape(rows, ML_V_W), c_new, n_new, m_new[:, 0, :ML_HEADS]


def _subhead_mean_square(t, bsum, split):
    t2 = t * t
    hi = t2.astype(BF16)
    lo = (t2 - hi.astype(F32)).astype(BF16) if split else None
    parts = []
    for j in range(t.shape[1] // V7X_MXU_DIM):
        sl = slice(j * V7X_MXU_DIM, (j + 1) * V7X_MXU_DIM)
        part = jnp.dot(hi[:, sl], bsum, preferred_element_type=F32)
        if split:
            part = part + jnp.dot(lo[:, sl], bsum, preferred_element_type=F32)
        parts.append(part)
    return jnp.concatenate(parts, axis=1)


def _rope(t, cos, sin_signed):
    lane = lax.broadcasted_iota(jnp.int32, (t.shape[0], V7X_LANES), 1)
    first = (lane % DA_HEAD_DIM) < ROT_HALF
    parts = []
    for j in range(t.shape[1] // V7X_LANES):
        tj = t[:, j * V7X_LANES:(j + 1) * V7X_LANES]
        partner = jnp.where(first, pltpu.roll(tj, V7X_LANES - ROT_HALF, axis=1),
                            pltpu.roll(tj, ROT_HALF, axis=1))
        parts.append(tj * cos + partner * sin_signed)
    return jnp.concatenate(parts, axis=1)


def _da_proj_kernel(*refs, for_prompt):
    if for_prompt:
        (x_ref, g_ref, w_ref, qg_ref, kg_ref, bsum_ref, cos_ref, sin_ref,
         wkvT_ref, kgc_ref, cosT_ref, sinT_ref, q_ref, v_ref, kb_ref, kT_ref, vT_ref) = refs
    else:
        (x_ref, g_ref, w_ref, qg_ref, kg_ref, bsum_ref, cos_ref, sin_ref,
         q_ref, k_ref, v_ref) = refs
    h = _rms_rows(x_ref[...], g_ref[...]).astype(BF16)
    z = jnp.dot(h, w_ref[...], preferred_element_type=F32)
    bsum = bsum_ref[...]
    cos = cos_ref[...]
    sin = sin_ref[...]
    split = not for_prompt
    q = z[:, :DA_W]
    q = _rope(q * lax.rsqrt(_subhead_mean_square(q, bsum, split) + EPS) * qg_ref[...], cos, sin)
    q_ref[...] = (q * SCORE_SCALE_LOG2).astype(BF16)
    v_ref[...] = z[:, z.shape[1] - v_ref.shape[1]:]
    if not for_prompt:
        k = z[:, DA_W:2 * DA_W]
        k_ref[...] = _rope(k * lax.rsqrt(_subhead_mean_square(k, bsum, split) + EPS)
                           * kg_ref[...], cos, sin)
        return
    nt = (((1,), (1,)), ((), ()))
    kvT = lax.dot_general(wkvT_ref[...], h, nt, preferred_element_type=F32)
    vT_ref[0] = kvT[DA_W:].astype(BF16)
    tm = kvT.shape[1]
    kT = kvT[:DA_W].reshape(DA_SUB, DA_HEAD_DIM, tm)
    ms = jnp.mean(kT * kT, axis=1, keepdims=True)
    kT = kT * lax.rsqrt(ms + EPS) * kgc_ref[...]
    x1 = kT[:, :ROT_HALF]
    x2 = kT[:, ROT_HALF:ROT_DIM]
    cosT = cosT_ref[...]
    sinT = sinT_ref[...]
    kT = jnp.concatenate([x1 * cosT - x2 * sinT, x2 * cosT + x1 * sinT, kT[:, ROT_DIM:]], axis=1)
    kT = kT.reshape(DA_W, tm)
    kT_ref[0] = kT
    kb_ref[...] = kT.T.astype(BF16)


def _rope_tables(pos):
    inv_freq = jnp.power(ROPE_THETA, -jnp.arange(0, ROT_DIM, 2, dtype=F32) / ROT_DIM)
    ang = pos.astype(F32)[:, None] * inv_freq[None, :]
    cos, sin = jnp.cos(ang), jnp.sin(ang)
    ones = jnp.ones((pos.shape[0], DA_HEAD_DIM - ROT_DIM), F32)
    cos64 = jnp.concatenate([cos, cos, ones], axis=1)
    sin64 = jnp.concatenate([-sin, sin, 0.0 * ones], axis=1)
    return jnp.tile(cos64, (1, 2)), jnp.tile(sin64, (1, 2)), cos.T, sin.T


def _da_proj_call(x, gain, w_qkv, q_gain, k_gain, pos, seq, *, for_prompt, row_target=512):
    rows, d = x.shape
    tm = _row_tile(seq, row_target)
    per_seq = seq // tm
    n_v = w_qkv.shape[1] - 2 * DA_W
    if for_prompt:
        w = jnp.concatenate([w_qkv[:, :DA_W], w_qkv[:, 2 * DA_W:]], axis=1).astype(BF16)
    else:
        w = w_qkv.astype(BF16)
    cos, sin, cosT, sinT = _rope_tables(pos)
    blk = jnp.arange(V7X_MXU_DIM) // DA_HEAD_DIM
    bsum = jnp.where(blk[:, None] == blk[None, :], 1.0 / DA_HEAD_DIM, 0.0).astype(BF16)
    qg = jnp.tile(q_gain.astype(F32), DA_SUB).reshape(1, DA_W)
    kg = jnp.tile(k_gain.astype(F32), DA_SUB).reshape(1, DA_W)
    row = lambda wd: pl.BlockSpec((tm, wd), lambda i: (i, 0))
    table = pl.BlockSpec((tm, V7X_LANES), lambda i: (i % per_seq, 0))
    in_specs = [row(d), _const_spec((1, d)), _const_spec(w.shape), _const_spec((1, DA_W)),
                _const_spec((1, DA_W)), _const_spec(bsum.shape), table, table]
    args = [x, gain.reshape(1, d), w, qg, kg, bsum, cos, sin]
    if for_prompt:
        assert n_v == DA_W
        wkvT = w_qkv[:, DA_W:].T.astype(BF16)
        kgc = k_gain.astype(F32).reshape(1, DA_HEAD_DIM, 1)
        tableT = pl.BlockSpec((ROT_HALF, tm), lambda i: (0, i % per_seq))
        by_seq = pl.BlockSpec((1, DA_W, tm), lambda i: (i // per_seq, 0, i % per_seq))
        in_specs += [_const_spec(wkvT.shape), _const_spec(kgc.shape), tableT, tableT]
        args += [wkvT, kgc, cosT, sinT]
        out_shape = [jax.ShapeDtypeStruct((rows, DA_W), BF16), jax.ShapeDtypeStruct((rows, n_v), F32),
                     jax.ShapeDtypeStruct((rows, DA_W), BF16),
                     jax.ShapeDtypeStruct((rows // seq, DA_W, seq), F32),
                     jax.ShapeDtypeStruct((rows // seq, n_v, seq), BF16)]
        out_specs = [row(DA_W), row(n_v), row(DA_W), by_seq, by_seq]
    else:
        out_shape = [jax.ShapeDtypeStruct((rows, DA_W), BF16),
                     jax.ShapeDtypeStruct((rows, DA_W), F32),
                     jax.ShapeDtypeStruct((rows, n_v), F32)]
        out_specs = [row(DA_W), row(DA_W), row(n_v)]
    return pl.pallas_call(
        functools.partial(_da_proj_kernel, for_prompt=for_prompt),
        out_shape=tuple(out_shape),
        grid=(rows // tm,),
        in_specs=in_specs,
        out_specs=tuple(out_specs),
        compiler_params=_params(("parallel",)),
        name="da_proj_prompt" if for_prompt else "da_proj_decode",
    )(*args)


def _lambda_value(lam_ref, lam_init):
    lp = lam_ref[...]
    s1 = jnp.sum(lp[0:1, :] * lp[1:2, :], axis=1, keepdims=True)
    s2 = jnp.sum(lp[2:3, :] * lp[3:4, :], axis=1, keepdims=True)
    return jnp.exp(s1) - jnp.exp(s2) + lam_init


def _da_attn_kernel(q_ref, k_ref, vT_ref, lam_ref, subln_ref, o_ref, q2_s, m_s, l_s, acc_s,
                    *, tq, tk, lam_init):
    i = pl.program_id(1)
    nt = (((1,), (1,)), ((), ()))
    lane = lax.broadcasted_iota(jnp.int32, (tq, V7X_LANES), 1)
    for h in range(DA_HEADS):
        qp = q_ref[0, :, h * DA_VDIM:(h + 1) * DA_VDIM]
        zero = jnp.zeros_like(qp)
        q2_s[h, 0:tq, :] = jnp.where(lane < DA_HEAD_DIM, qp, zero)
        q2_s[h, tq:2 * tq, :] = jnp.where(lane >= DA_HEAD_DIM, qp, zero)
    m_s[...] = jnp.full(m_s.shape, -jnp.inf, F32)
    l_s[...] = jnp.zeros_like(l_s)
    acc_s[...] = jnp.zeros_like(acc_s)

    key_l = lax.broadcasted_iota(jnp.int32, (tk, 2 * tq), 0)
    qry_g = i * tq + lax.broadcasted_iota(jnp.int32, (tk, 2 * tq), 1) % tq

    def block(j, masked):
        start = pl.multiple_of(j * tk, tk)

        def scores(h):
            kb = k_ref[0, pl.ds(start, tk), h * DA_VDIM:(h + 1) * DA_VDIM]
            return lax.dot_general(kb, q2_s[h], nt, preferred_element_type=F32)

        ahead = [scores(h) for h in range(ATTN_HEADS_AHEAD)]
        for h in range(DA_HEADS):
            hs = slice(h * DA_VDIM, (h + 1) * DA_VDIM)
            s = ahead.pop(0)
            if h + ATTN_HEADS_AHEAD < DA_HEADS:
                ahead.append(scores(h + ATTN_HEADS_AHEAD))
            if masked:
                s = jnp.where(j * tk + key_l <= qry_g, s, -jnp.inf)
            m_old = m_s[h]
            m_new = jnp.maximum(m_old, jnp.max(s, axis=0, keepdims=True))
            alpha = jnp.exp2(m_old - m_new)
            p = jnp.exp2(s - m_new)
            l_s[h] = alpha * l_s[h] + jnp.sum(p, axis=0, keepdims=True)
            vTb = vT_ref[0, hs, pl.ds(start, tk)]
            acc_s[h] = alpha * acc_s[h] + jnp.dot(vTb, p.astype(BF16),
                                                  preferred_element_type=F32)
            m_s[h] = m_new

    n_full = (i * tq) // tk

    def full_block(j, carry):
        block(j, False)
        return carry

    lax.fori_loop(0, n_full, full_block, 0)
    block(n_full, True)

    lam = _lambda_value(lam_ref, lam_init)
    for h in range(DA_HEADS):
        oT = acc_s[h] / l_s[h]
        oT = oT[:, :tq] - lam * oT[:, tq:]
        ms = jnp.mean(oT * oT, axis=0, keepdims=True)
        oT = oT * lax.rsqrt(ms + EPS) * (subln_ref[...] * (1.0 - lam_init))
        o_ref[0, :, h * DA_VDIM:(h + 1) * DA_VDIM] = oT.T.astype(BF16)


def _da_attn_call(q, kb, vT, lam_p, subln, batch, seq, lam_init):
    tk = _row_tile(seq, ATTN_KV_BLOCK)
    tq = min(_row_tile(seq, ATTN_Q_BLOCK), tk)
    assert tk % tq == 0
    w = q.shape[1]
    q3, k3 = (t.reshape(batch, seq, t.shape[1]) for t in (q, kb))
    out = pl.pallas_call(
        functools.partial(_da_attn_kernel, tq=tq, tk=tk, lam_init=lam_init),
        out_shape=jax.ShapeDtypeStruct((batch, seq, vT.shape[1]), BF16),
        grid=(batch, seq // tq),
        in_specs=[pl.BlockSpec((1, tq, w), lambda b, i: (b, i, 0)),
                  pl.BlockSpec((1, seq, w), lambda b, i: (b, 0, 0)),
                  pl.BlockSpec((1, vT.shape[1], seq), lambda b, i: (b, 0, 0)),
                  pl.BlockSpec(lam_p.shape, lambda b, i: (0, 0)),
                  pl.BlockSpec((DA_VDIM, 1), lambda b, i: (0, 0))],
        out_specs=pl.BlockSpec((1, tq, vT.shape[1]), lambda b, i: (b, i, 0)),
        scratch_shapes=[pltpu.VMEM((DA_HEADS, 2 * tq, DA_VDIM), BF16),
                        pltpu.VMEM((DA_HEADS, 1, 2 * tq), F32),
                        pltpu.VMEM((DA_HEADS, 1, 2 * tq), F32),
                        pltpu.VMEM((DA_HEADS, DA_VDIM, 2 * tq), F32)],
        compiler_params=_params(("parallel", "arbitrary")),
        name="da_attn_prompt",
    )(q3, k3, vT, lam_p.astype(F32), subln.reshape(DA_VDIM, 1).astype(F32))
    return out.reshape(batch * seq, vT.shape[1])


def _da_decode_kernel(pt_ref, q_ref, kn_ref, vn_ref, lam_ref, subln_ref, expand_ref, *rest,
                      pages, lam_init):
    k_refs = rest[:pages]
    v_refs = rest[pages:2 * pages]
    o_ref, qbd_s, m_s, l_s, acc_s = rest[2 * pages:]
    j = pl.program_id(1)
    flat = (DA_SUB * DA_HEAD_DIM, PAGE_SIZE)

    @pl.when(j == 0)
    def _():
        q = q_ref[0]
        sub = lax.broadcasted_iota(jnp.int32, (DA_SUB, DA_W), 0)
        col = lax.broadcasted_iota(jnp.int32, (DA_SUB, DA_W), 1)
        qb = jnp.broadcast_to(q.astype(F32), (DA_SUB, DA_W))
        qbd_s[...] = jnp.where(col // DA_HEAD_DIM == sub, qb, 0.0).astype(BF16)
        m_s[...] = jnp.full(m_s.shape, -jnp.inf, F32)
        l_s[...] = jnp.zeros_like(l_s)
        acc_s[...] = jnp.zeros_like(acc_s)

    qbd = qbd_s[...]
    s = jnp.concatenate(
        [jnp.dot(qbd, k_refs[i][...].reshape(flat).astype(BF16), preferred_element_type=F32)
         for i in range(pages)], axis=1)
    m_old = m_s[...]
    m_new = jnp.maximum(m_old, jnp.max(s, axis=1, keepdims=True))
    alpha = jnp.exp2(m_old - m_new)
    p = jnp.exp2(s - m_new)
    l_s[...] = alpha * l_s[...] + jnp.sum(p, axis=1, keepdims=True)
    p_rows = jnp.concatenate([p[:, i * PAGE_SIZE:(i + 1) * PAGE_SIZE] for i in range(pages)],
                             axis=0).astype(BF16)
    w = jnp.dot(p_rows, expand_ref[...], preferred_element_type=F32)
    sub = lax.broadcasted_iota(jnp.int32, w.shape, 0) % DA_SUB
    head = lax.broadcasted_iota(jnp.int32, w.shape, 1) % DA_HEADS
    w = jnp.where(head == sub // 2, w, 0.0).astype(BF16)
    pv = jnp.zeros(acc_s.shape, F32)
    for i in range(pages):
        pv += jnp.dot(w[i * DA_SUB:(i + 1) * DA_SUB], v_refs[i][...].reshape(flat).astype(BF16),
                      preferred_element_type=F32)
    acc_s[...] = alpha * acc_s[...] + pv
    m_s[...] = m_new

    @pl.when(j == pl.num_programs(1) - 1)
    def _():
        lam = _lambda_value(lam_ref, lam_init)
        m_p = m_s[...]
        s_self = jnp.sum(qbd.astype(F32) * kn_ref[0], axis=1, keepdims=True)
        m_f = jnp.maximum(m_p, s_self)
        a_f = jnp.exp2(m_p - m_f)
        p_self = jnp.exp2(s_self - m_f)
        l_f = a_f * l_s[...] + p_self
        past = a_f * acc_s[...]
        v_self = vn_ref[0]
        for h in range(DA_HEADS):
            o2 = ((past[2 * h:2 * h + 2] + p_self[2 * h:2 * h + 2] * v_self[h:h + 1])
                  / l_f[2 * h:2 * h + 2])
            oh = o2[0:1] - lam * o2[1:2]
            oh = _rms_rows(oh, subln_ref[...]) * (1.0 - lam_init)
            o_ref[0, h:h + 1, :] = oh.astype(BF16)


def _da_decode_call(q, k_new, v_new, cache_k, cache_v, layer, page_table, lam_p, subln, lam_init):
    rows = q.shape[0]
    n_pages = page_table.shape[1]
    pages = DECODE_PAGES_PER_STEP
    while n_pages % pages:
        pages //= 2
    ck = jnp.transpose(cache_k, (0, 1, 3, 4, 2))
    per_b = lambda a: pl.BlockSpec((1,) + a.shape[1:], lambda b, j, pt: (b, 0, 0))
    page_index = lambda i: (lambda b, j, pt: (layer, pt[b, j * pages + i], 0, 0, 0))
    k_spec = lambda i: pl.BlockSpec((None, None, DA_SUB, DA_HEAD_DIM, PAGE_SIZE), page_index(i))
    v_spec = lambda i: pl.BlockSpec((None, None, PAGE_SIZE, DA_HEADS, DA_VDIM), page_index(i))
    token = jnp.arange(PAGE_SIZE * DA_HEADS) // DA_HEADS
    expand = (token[None, :] == jnp.arange(PAGE_SIZE)[:, None]).astype(BF16)

    q3 = q.reshape(rows, 1, DA_W)
    kn3 = k_new.reshape(rows, 1, DA_W)
    vn3 = v_new.reshape(rows, DA_HEADS, DA_VDIM)
    grid_spec = pltpu.PrefetchScalarGridSpec(
        num_scalar_prefetch=1,
        grid=(rows, n_pages // pages),
        in_specs=[per_b(q3), per_b(kn3), per_b(vn3),
                  pl.BlockSpec(lam_p.shape, lambda b, j, pt: (0, 0)),
                  pl.BlockSpec((1, DA_VDIM), lambda b, j, pt: (0, 0)),
                  pl.BlockSpec(expand.shape, lambda b, j, pt: (0, 0))]
        + [k_spec(i) for i in range(pages)] + [v_spec(i) for i in range(pages)],
        out_specs=pl.BlockSpec((1, DA_HEADS, DA_VDIM), lambda b, j, pt: (b, 0, 0)),
        scratch_shapes=[pltpu.VMEM((DA_SUB, DA_W), BF16), pltpu.VMEM((DA_SUB, 1), F32),
                        pltpu.VMEM((DA_SUB, 1), F32), pltpu.VMEM((DA_SUB, DA_VDIM), F32)],
    )
    out = pl.pallas_call(
        functools.partial(_da_decode_kernel, pages=pages, lam_init=lam_init),
        out_shape=jax.ShapeDtypeStruct((rows, DA_HEADS, DA_VDIM), BF16),
        grid_spec=grid_spec,
        compiler_params=_params(("parallel", "arbitrary")),
        name="da_attn_decode",
    )(page_table, q3, kn3, vn3, lam_p.astype(F32), subln.reshape(1, DA_VDIM).astype(F32), expand,
      *([ck] * pages), *([cache_v] * pages))
    return out.reshape(rows, DA_HEADS * DA_VDIM)


def kernel(x_prompt, x_sample, state_C, state_n, state_m, cache_k, cache_v, page_table,
           ffn1_norm, ffn1_w_gu, ffn1_w_down, mix_norm, ffn2_norm, ffn2_w_gu, ffn2_w_down,
           ml_w_in, ml_gate_bias, ml_out_norm, ml_w_out,
           da_w_qkv, da_q_norm, da_k_norm, da_lambda, da_subln, da_w_out):
    depth = ffn1_norm.shape[0]
    bp, lp, d = x_prompt.shape
    bs, ls, _ = x_sample.shape
    assert ls == 1
    past_len = page_table.shape[1] * PAGE_SIZE
    pos_p = jnp.arange(lp, dtype=jnp.int32)
    pos_s = past_len + jnp.arange(ls, dtype=jnp.int32)

    xp = x_prompt.reshape(bp * lp, d)
    xs = x_sample.reshape(bs * ls, d)
    ml_p, ml_s, kv_p, kv_s = [], [], [], []
    ffn1 = (ffn1_norm.astype(F32).reshape(depth, 1, d), ffn1_w_gu.astype(BF16),
            ffn1_w_down.astype(BF16))
    ffn2 = (ffn2_norm.astype(F32).reshape(depth, 1, d), ffn2_w_gu.astype(BF16),
            ffn2_w_down.astype(BF16))
    for i in range(depth):
        j = i // N_MIXERS
        xp, xs = _ffn_call(xp, xs, *ffn1, i)
        if i % N_MIXERS == 0:
            q, kT, v, o, gates = _ml_proj_prompt_call(xp, mix_norm[i], ml_w_in[j], ml_gate_bias[j],
                                                      bp, lp)
            act_p, c_p, n_p, m_p = _ml_recur_call(q, kT, v, o, gates, ml_out_norm[j], bp, lp)
            ml_p.append((c_p, n_p, m_p))
            z, zT, gts = _ml_proj_decode_call(xs, mix_norm[i], ml_w_in[j], ml_gate_bias[j])
            act_s, c_s, n_s, m_s = _ml_decode_call(z, zT, gts, state_C[j], state_n[j], state_m[j],
                                                   ml_out_norm[j])
            ml_s.append((c_s, n_s, m_s))
            w_out = ml_w_out[j].astype(BF16)
        else:
            lam_init = _lambda_init(i)
            qp, vp, kpb, kpT, vpT = _da_proj_call(xp, mix_norm[i], da_w_qkv[j], da_q_norm[j],
                                                  da_k_norm[j], pos_p, lp, for_prompt=True)
            act_p = _da_attn_call(qp, kpb, vpT, da_lambda[j], da_subln[j], bp, lp, lam_init)
            kp = jnp.transpose(kpT.reshape(bp, DA_SUB, DA_HEAD_DIM, lp), (0, 3, 1, 2))
            kv_p.append((kp, vp.reshape(bp, lp, DA_HEADS, DA_VDIM)))
            qs, ks, vs = _da_proj_call(xs, mix_norm[i], da_w_qkv[j], da_q_norm[j], da_k_norm[j],
                                       jnp.repeat(pos_s, bs), bs, for_prompt=False)
            act_s = _da_decode_call(qs, ks, vs, cache_k, cache_v, j, page_table,
                                    da_lambda[j], da_subln[j], lam_init)
            kv_s.append((ks.reshape(bs, ls, DA_SUB, DA_HEAD_DIM),
                         vs.reshape(bs, ls, DA_HEADS, DA_VDIM)))
            w_out = da_w_out[j].astype(BF16)
        xp, xs = _ffn_call(xp, xs, *ffn2, i, pre=(act_p, act_s, w_out))

    stack = lambda items, k: jnp.stack([it[k] for it in items])
    return (xp.reshape(bp, lp, d), xs.reshape(bs, ls, d),
            stack(ml_p, 0), stack(ml_p, 1), stack(ml_p, 2),
            stack(ml_s, 0), stack(ml_s, 1), stack(ml_s, 2),
            stack(kv_p, 0), stack(kv_p, 1), stack(kv_s, 0), stack(kv_s, 1))
```

```python
import functools
import math

import jax
import jax.numpy as jnp
from jax import lax
from jax.experimental import pallas as pl
from jax.experimental.pallas import tpu as pltpu

F32 = jnp.float32
BF16 = jnp.bfloat16

EPS = 1e-6
ML_HEADS = 4
ML_DK = 128
ML_DV = 256
ML_QK_W = ML_HEADS * ML_DK
ML_V_W = ML_HEADS * ML_DV
GATE_SOFTCAP = 15.0
DA_HEAD_DIM = 64
DA_HEADS = 8
DA_SUB = 2 * DA_HEADS
DA_VDIM = 128
DA_W = DA_SUB * DA_HEAD_DIM
ROPE_THETA = 500000.0
ROT_DIM = DA_HEAD_DIM // 4
ROT_HALF = ROT_DIM // 2
PAGE_SIZE = 128
N_MIXERS = 2
SCORE_SCALE_LOG2 = (DA_HEAD_DIM ** -0.5) * math.log2(math.e)

V7X_LANES = 128
V7X_MXU_DIM = 256
V7X_VMEM_LIMIT_BYTES = 56 * 1024 * 1024

FFN_CHUNK = 256
ML_CHUNK = 128
ATTN_Q_BLOCK = 256
ATTN_KV_BLOCK = 256
ATTN_HEADS_AHEAD = 2
ATTN_SUM_ROWS = 16
DECODE_PAGES_PER_STEP = 16
ML_DECODE_SEQS_PER_STEP = 4


def _lambda_init(layer):
    return 0.8 - 0.6 * math.exp(-0.3 * layer)


def _const_spec(shape):
    zeros = (0,) * len(shape)
    return pl.BlockSpec(shape, lambda *_: zeros, pipeline_mode=pl.Buffered(1))


def _params(semantics):
    return pltpu.CompilerParams(dimension_semantics=semantics,
                                vmem_limit_bytes=V7X_VMEM_LIMIT_BYTES)


def _row_tile(rows, target):
    t = min(rows, target)
    while rows % t:
        t //= 2
    return t


def _rms_rows(x, gain):
    ms = jnp.mean(x * x, axis=-1, keepdims=True)
    return x * lax.rsqrt(ms + EPS) * gain


def _ffn_kernel(*refs, has_pre, d_ff, tf):
    if has_pre:
        x_ref, a_ref, xs_ref, as_ref, wo_ref, g_ref, wgu_ref, wd_ref, o_ref, os_ref, h_scr = refs
    else:
        x_ref, xs_ref, g_ref, wgu_ref, wd_ref, o_ref, os_ref, h_scr = refs
        a_ref = as_ref = None

    def run(x_in, a_in, out):
        rows = x_in.shape[0]
        x = x_in[...]
        if has_pre:
            x = x + jnp.dot(a_in[...], wo_ref[...], preferred_element_type=F32)
        h_scr[0:rows, :] = _rms_rows(x, g_ref[...]).astype(BF16)
        out[...] = x
        for c in range(d_ff // tf):
            h = h_scr[0:rows, :]
            g = jnp.dot(h, wgu_ref[:, c * tf:(c + 1) * tf], preferred_element_type=F32)
            u = jnp.dot(h, wgu_ref[:, d_ff + c * tf:d_ff + (c + 1) * tf],
                        preferred_element_type=F32)
            act = (0.5 * g * jax.nn.sigmoid(g) * u).astype(BF16)
            out[...] += jnp.dot(act, wd_ref[c * tf:(c + 1) * tf, :], preferred_element_type=F32)

    run(x_ref, a_ref, o_ref)

    @pl.when(pl.program_id(0) == pl.num_programs(0) - 1)
    def _():
        run(xs_ref, as_ref, os_ref)


def _ffn_call(x, xs, gains, w_gu, w_down, layer, pre=None, *, row_target=512):
    rows, d = x.shape
    d_ff = w_down.shape[1]
    tf = FFN_CHUNK
    while d_ff % tf:
        tf //= 2
    tm = _row_tile(rows, row_target)
    assert xs.shape[0] <= tm
    row_spec = pl.BlockSpec((tm, d), lambda i: (i, 0))
    whole = lambda a: _const_spec(a.shape)
    of_layer = lambda a: pl.BlockSpec((None,) + a.shape[1:], lambda i: (layer, 0, 0),
                                      pipeline_mode=pl.Buffered(1))
    in_specs, args = [row_spec], [x]
    if pre is not None:
        a, a_s, wo = pre
        in_specs += [pl.BlockSpec((tm, a.shape[1]), lambda i: (i, 0)), whole(xs), whole(a_s),
                     whole(wo)]
        args += [a, xs, a_s, wo]
    else:
        in_specs.append(whole(xs))
        args.append(xs)
    in_specs += [of_layer(gains), of_layer(w_gu), of_layer(w_down)]
    args += [gains, w_gu, w_down]
    return pl.pallas_call(
        functools.partial(_ffn_kernel, has_pre=pre is not None, d_ff=d_ff, tf=tf),
        out_shape=(jax.ShapeDtypeStruct((rows, d), F32), jax.ShapeDtypeStruct(xs.shape, F32)),
        grid=(rows // tm,),
        in_specs=in_specs,
        out_specs=(row_spec, pl.BlockSpec(xs.shape, lambda i: (0, 0))),
        scratch_shapes=[pltpu.VMEM((tm, d), BF16)],
        compiler_params=_params(("arbitrary",)),
        name="ffn_pre" if pre is not None else "ffn",
    )(*args)


def _activate_gates(g, is_forget):
    g = GATE_SOFTCAP * jnp.tanh(g / GATE_SOFTCAP)
    log_sig = jnp.minimum(g, 0.0) - jnp.log1p(jnp.exp(-jnp.abs(g)))
    return jnp.where(is_forget, log_sig, g)


def _ml_proj_prompt_kernel(x_ref, g_ref, wqvo_ref, wkT_ref, wgT_ref, gb_ref,
                           q_ref, kT_ref, v_ref, o_ref, gates_ref):
    h = _rms_rows(x_ref[...], g_ref[...]).astype(BF16)
    z = jnp.dot(h, wqvo_ref[...], preferred_element_type=F32)
    q_ref[...] = z[:, :ML_QK_W].astype(BF16)
    v_ref[...] = z[:, ML_QK_W:ML_QK_W + ML_V_W].astype(BF16)
    o_ref[...] = z[:, ML_QK_W + ML_V_W:]
    nt = (((1,), (1,)), ((), ()))
    kT = lax.dot_general(wkT_ref[...], h, nt, preferred_element_type=F32)
    kT_ref[0] = (kT * (ML_DK ** -0.5)).astype(BF16)
    gT = lax.dot_general(wgT_ref[...], h, nt, preferred_element_type=F32) + gb_ref[...]
    row = lax.broadcasted_iota(jnp.int32, gT.shape, 0)
    gates_ref[0] = _activate_gates(gT, row >= ML_HEADS)


def _ml_proj_prompt_call(x, gain, w_in, gate_bias, batch, seq, *, row_target=512):
    rows, d = x.shape
    tm = _row_tile(seq, row_target)
    per_seq = seq // tm
    ng = 2 * ML_HEADS
    wqvo = jnp.concatenate([w_in[:, :ML_QK_W], w_in[:, 2 * ML_QK_W:2 * ML_QK_W + 2 * ML_V_W]],
                           axis=1).astype(BF16)
    wkT = w_in[:, ML_QK_W:2 * ML_QK_W].T.astype(BF16)
    wgT = w_in[:, 2 * ML_QK_W + 2 * ML_V_W:].T.astype(BF16)
    row = lambda w: pl.BlockSpec((tm, w), lambda i: (i, 0))
    by_seq = lambda r: pl.BlockSpec((1, r, tm), lambda i: (i // per_seq, 0, i % per_seq))
    return pl.pallas_call(
        _ml_proj_prompt_kernel,
        out_shape=(jax.ShapeDtypeStruct((rows, ML_QK_W), BF16),
                   jax.ShapeDtypeStruct((batch, ML_QK_W, seq), BF16),
                   jax.ShapeDtypeStruct((rows, ML_V_W), BF16),
                   jax.ShapeDtypeStruct((rows, ML_V_W), F32),
                   jax.ShapeDtypeStruct((batch, ng, seq), F32)),
        grid=(rows // tm,),
        in_specs=[row(d), _const_spec((1, d)), _const_spec(wqvo.shape), _const_spec(wkT.shape),
                  _const_spec(wgT.shape), _const_spec((ng, 1))],
        out_specs=(row(ML_QK_W), by_seq(ML_QK_W), row(ML_V_W), row(ML_V_W), by_seq(ng)),
        compiler_params=_params(("parallel",)),
        name="ml_proj_prompt",
    )(x, gain.reshape(1, d), wqvo, wkT, wgT, gate_bias.reshape(ng, 1).astype(F32))


def _ml_recur_kernel(q_ref, kT_ref, v_ref, o_ref, gates_ref, gain_ref,
                     act_ref, c_out, n_out, m_out, c_s, n_s, m_s, *, chunk):
    j = pl.program_id(1)

    @pl.when(j == 0)
    def _():
        c_s[...] = jnp.zeros_like(c_s)
        n_s[...] = jnp.zeros_like(n_s)
        m_s[...] = jnp.zeros_like(m_s)

    gates = gates_ref[0]
    lane = lax.broadcasted_iota(jnp.int32, gates.shape, 1)
    cum = gates
    step = 1
    while step < chunk:
        cum = cum + jnp.where(lane >= step, pltpu.roll(cum, step, axis=1), 0.0)
        step *= 2
    t_idx = lax.broadcasted_iota(jnp.int32, (chunk, chunk), 0)
    s_idx = lax.broadcasted_iota(jnp.int32, (chunk, chunk), 1)
    causal = s_idx <= t_idx
    nt = (((1,), (1,)), ((), ()))

    heads = range(ML_HEADS)
    dot = functools.partial(jnp.dot, preferred_element_type=F32)
    qh = [q_ref[0, :, h * ML_DK:(h + 1) * ML_DK] for h in heads]
    kTh = [kT_ref[0, h * ML_DK:(h + 1) * ML_DK, :] for h in heads]
    vh = [v_ref[0, :, h * ML_DV:(h + 1) * ML_DV] for h in heads]
    c_old = [c_s[h] for h in heads]
    n_row = [n_s[h:h + 1, :] for h in heads]
    m_prev = [m_s[h:h + 1, 0:1] for h in heads]
    qk = [dot(qh[h], kTh[h]) for h in heads]
    qc = [dot(qh[h], c_old[h].astype(BF16)) for h in heads]
    qn = [jnp.sum(qh[h].astype(F32) * n_row[h], axis=1, keepdims=True) for h in heads]

    li_r = [gates[h:h + 1, :] for h in heads]
    lf_r = [gates[ML_HEADS + h:ML_HEADS + h + 1, :] for h in heads]
    b_r = [cum[ML_HEADS + h:ML_HEADS + h + 1, :] for h in heads]
    b_last = [b[:, chunk - 1:chunk] for b in b_r]
    b_c = [jnp.sum(jnp.where(causal, lf, 0.0), axis=1, keepdims=True) for lf in lf_r]
    d_mat = [jnp.where(causal, b_c[h] - b_r[h] + li_r[h], -jnp.inf) for h in heads]
    a = [b_c[h] + m_prev[h] for h in heads]
    mt = [jnp.maximum(a[h], jnp.max(d_mat[h], axis=1, keepdims=True)) for h in heads]
    s = [qk[h] * jnp.exp(d_mat[h] - mt[h]) for h in heads]
    sv = [dot(s[h].astype(BF16), vh[h]) for h in heads]

    g_r = [b_last[h] - b_r[h] + li_r[h] for h in heads]
    m_new = [jnp.maximum(b_last[h] + m_prev[h], jnp.max(g_r[h], axis=1, keepdims=True))
             for h in heads]
    decay = [jnp.exp(b_last[h] + m_prev[h] - m_new[h]) for h in heads]
    wk = [jnp.exp(g_r[h] - m_new[h]) for h in heads]
    kv = [dot((kTh[h].astype(F32) * wk[h]).astype(BF16), vh[h]) for h in heads]
    nk = [lax.dot_general(wk[h].astype(BF16), kTh[h], nt, preferred_element_type=F32)
          for h in heads]

    for h in heads:
        w_inter = jnp.exp(a[h] - mt[h])
        num = w_inter * qc[h] + sv[h]
        den = w_inter * qn[h] + jnp.sum(s[h], axis=1, keepdims=True)
        hid = num / jnp.maximum(jnp.abs(den), jnp.exp(-mt[h]))
        hn = _rms_rows(hid, gain_ref[h:h + 1, :])
        og = jax.nn.sigmoid(o_ref[0, :, h * ML_DV:(h + 1) * ML_DV])
        act_ref[0, :, h * ML_DV:(h + 1) * ML_DV] = (hn * og).astype(BF16)
    for h in heads:
        c_s[h] = decay[h] * c_old[h] + kv[h]
        n_s[h:h + 1, :] = decay[h] * n_row[h] + nk[h]
        m_s[h:h + 1, :] = jnp.broadcast_to(m_new[h], (1, V7X_LANES))

    @pl.when(j == pl.num_programs(1) - 1)
    def _():
        c_out[0] = c_s[...]
        n_out[0] = n_s[0:ML_HEADS, :]
        lane_m = lax.broadcasted_iota(jnp.int32, (1, V7X_LANES), 1)
        m_row = jnp.zeros((1, V7X_LANES), F32)
        for h in range(ML_HEADS):
            m_row = jnp.where(lane_m == h, m_s[h:h + 1, :], m_row)
        m_out[0] = m_row


def _ml_recur_call(q, kT, v, o, gates, out_gain, batch, seq):
    chunk = _row_tile(seq, ML_CHUNK)
    n_steps = seq // chunk
    ng = 2 * ML_HEADS
    q3 = q.reshape(batch, seq, ML_QK_W)
    v3 = v.reshape(batch, seq, ML_V_W)
    o3 = o.reshape(batch, seq, ML_V_W)
    tok = lambda w: pl.BlockSpec((1, chunk, w), lambda b, j: (b, j, 0))
    col = lambda r: pl.BlockSpec((1, r, chunk), lambda b, j: (b, 0, j))
    per_b = lambda *s: pl.BlockSpec((1,) + s, lambda b, j: (b,) + (0,) * len(s))
    act, c_new, n_new, m_new = pl.pallas_call(
        functools.partial(_ml_recur_kernel, chunk=chunk),
        out_shape=(jax.ShapeDtypeStruct((batch, seq, ML_V_W), BF16),
                   jax.ShapeDtypeStruct((batch, ML_HEADS, ML_DK, ML_DV), F32),
                   jax.ShapeDtypeStruct((batch, ML_HEADS, ML_DK), F32),
                   jax.ShapeDtypeStruct((batch, 1, V7X_LANES), F32)),
        grid=(batch, n_steps),
        in_specs=[tok(ML_QK_W), col(ML_QK_W), tok(ML_V_W), tok(ML_V_W), col(ng),
                  pl.BlockSpec((ML_HEADS, ML_DV), lambda b, j: (0, 0))],
        out_specs=(tok(ML_V_W), per_b(ML_HEADS, ML_DK, ML_DV), per_b(ML_HEADS, ML_DK),
                   per_b(1, V7X_LANES)),
        scratch_shapes=[pltpu.VMEM((ML_HEADS, ML_DK, ML_DV), F32),
                        pltpu.VMEM((8, ML_DK), F32),
                        pltpu.VMEM((8, V7X_LANES), F32)],
        compiler_params=_params(("parallel", "arbitrary")),
        name="ml_recur",
    )(q3, kT, v3, o3, gates, out_gain.astype(F32))
    return act.reshape(batch * seq, ML_V_W), c_new, n_new, m_new[:, 0, :ML_HEADS]


def _ml_proj_decode_kernel(x_ref, g_ref, w_ref, wqkT_ref, wg_ref, gb_ref, z_ref, zT_ref, gates_ref):
    h = _rms_rows(x_ref[...], g_ref[...]).astype(BF16)
    z = jnp.dot(h, w_ref[...], preferred_element_type=F32)
    col = lax.broadcasted_iota(jnp.int32, z.shape, 1)
    is_k = (col >= ML_QK_W) & (col < 2 * ML_QK_W)
    z_ref[...] = jnp.where(is_k, z * (ML_DK ** -0.5), z)
    nt = (((1,), (1,)), ((), ()))
    zT = lax.dot_general(wqkT_ref[...], h, nt, preferred_element_type=F32)
    rowi = lax.broadcasted_iota(jnp.int32, zT.shape, 0)
    zT_ref[...] = jnp.where(rowi >= ML_QK_W, zT * (ML_DK ** -0.5), zT)
    g = jnp.dot(h, wg_ref[...], preferred_element_type=F32) + gb_ref[...]
    lane = lax.broadcasted_iota(jnp.int32, g.shape, 1)
    gates_ref[...] = _activate_gates(g, lane >= ML_HEADS)


def _ml_proj_decode_call(x, gain, w_in, gate_bias):
    rows, d = x.shape
    ng = 2 * ML_HEADS
    w_main = w_in[:, :2 * ML_QK_W + 2 * ML_V_W].astype(BF16)
    wqkT = w_in[:, :2 * ML_QK_W].T.astype(BF16)
    wg = jnp.pad(w_in[:, 2 * ML_QK_W + 2 * ML_V_W:], ((0, 0), (0, V7X_LANES - ng))).astype(BF16)
    gb = jnp.pad(gate_bias.astype(F32), (0, V7X_LANES - ng)).reshape(1, V7X_LANES)
    full = lambda a: _const_spec(a.shape)
    args = (x, gain.reshape(1, d), w_main, wqkT, wg, gb)
    out_shape = (jax.ShapeDtypeStruct((rows, w_main.shape[1]), F32),
                 jax.ShapeDtypeStruct((2 * ML_QK_W, rows), F32),
                 jax.ShapeDtypeStruct((rows, V7X_LANES), F32))
    return pl.pallas_call(
        _ml_proj_decode_kernel,
        out_shape=out_shape,
        grid=(1,),
        in_specs=[full(a) for a in args],
        out_specs=tuple(pl.BlockSpec(s.shape, lambda i: (0, 0)) for s in out_shape),
        compiler_params=_params(("arbitrary",)),
        name="ml_proj_decode",
    )(*args)


def _ml_decode_kernel(z_ref, zT_ref, gates_ref, c_ref, n_ref, m_ref, gain_ref,
                      act_ref, c_out, n_out, m_out, *, seqs):
    for r in range(seqs):
        _ml_decode_one(pl.program_id(0) * seqs + r, r, z_ref, zT_ref, gates_ref, c_ref, n_ref,
                       m_ref, gain_ref, act_ref, c_out, n_out, m_out)


def _ml_decode_one(b, r, z_ref, zT_ref, gates_ref, c_ref, n_ref, m_ref, gain_ref,
                   act_ref, c_out, n_out, m_out):
    z = z_ref[pl.ds(b, 1), :]
    g = gates_ref[pl.ds(b, 1), :]
    m_in = m_ref[pl.ds(b, 1), :]
    zT = zT_ref[...]
    pick = lax.broadcasted_iota(jnp.int32, zT.shape, 1) == b
    z_col = jnp.sum(jnp.where(pick, zT, 0.0), axis=1, keepdims=True)
    lane_m = lax.broadcasted_iota(jnp.int32, (1, V7X_LANES), 1)
    m_row = jnp.zeros((1, V7X_LANES), F32)

    for h in range(ML_HEADS):
        li = g[:, h:h + 1]
        lf = g[:, ML_HEADS + h:ML_HEADS + h + 1]
        m_prev = m_in[:, h:h + 1]
        q_row = z[:, h * ML_DK:(h + 1) * ML_DK]
        k_row = z[:, ML_QK_W + h * ML_DK:ML_QK_W + (h + 1) * ML_DK]
        v_row = z[:, 2 * ML_QK_W + h * ML_DV:2 * ML_QK_W + (h + 1) * ML_DV]
        o_row = z[:, 2 * ML_QK_W + ML_V_W + h * ML_DV:2 * ML_QK_W + ML_V_W + (h + 1) * ML_DV]
        q_col = z_col[h * ML_DK:(h + 1) * ML_DK, :]
        k_col = z_col[ML_QK_W + h * ML_DK:ML_QK_W + (h + 1) * ML_DK, :]
        c_old = c_ref[r, h]
        n_row = n_ref[r, h:h + 1, :]

        a = lf + m_prev
        mt = jnp.maximum(a, li)
        w_inter = jnp.exp(a - mt)
        s = jnp.sum(q_row * k_row, axis=1, keepdims=True) * jnp.exp(li - mt)
        num = w_inter * jnp.sum(q_col * c_old, axis=0, keepdims=True) + s * v_row
        den = w_inter * jnp.sum(q_row * n_row, axis=1, keepdims=True) + s
        hid = num / jnp.maximum(jnp.abs(den), jnp.exp(-mt))
        hn = _rms_rows(hid, gain_ref[h:h + 1, :])
        act_ref[r, :, h * ML_DV:(h + 1) * ML_DV] = (hn * jax.nn.sigmoid(o_row)).astype(BF16)

        m_new = jnp.maximum(a, li)
        decay = jnp.exp(a - m_new)
        wk = jnp.exp(li - m_new)
        c_out[r, h] = decay * c_old + (wk * k_col) * v_row
        n_out[r, h:h + 1, :] = decay * n_row + wk * k_row
        m_row = jnp.where(lane_m == h, m_new, m_row)
    m_out[r] = m_row


def _ml_decode_call(z, zT, gates, state_c, state_n, state_m, out_gain):
    rows = z.shape[0]
    seqs = _row_tile(rows, ML_DECODE_SEQS_PER_STEP)
    full = lambda a: _const_spec(a.shape)
    per_b = lambda *s: pl.BlockSpec((seqs,) + s, lambda b: (b,) + (0,) * len(s))
    act, c_new, n_new, m_new = pl.pallas_call(
        functools.partial(_ml_decode_kernel, seqs=seqs),
        out_shape=(jax.ShapeDtypeStruct((rows, 1, ML_V_W), BF16),
                   jax.ShapeDtypeStruct(state_c.shape, F32),
                   jax.ShapeDtypeStruct(state_n.shape, F32),
                   jax.ShapeDtypeStruct((rows, 1, V7X_LANES), F32)),
        grid=(rows // seqs,),
        in_specs=[full(z), full(zT), full(gates), per_b(ML_HEADS, ML_DK, ML_DV),
                  per_b(ML_HEADS, ML_DK), full(state_m), full(out_gain)],
        out_specs=(per_b(1, ML_V_W), per_b(ML_HEADS, ML_DK, ML_DV), per_b(ML_HEADS, ML_DK),
                   per_b(1, V7X_LANES)),
        compiler_params=_params(("arbitrary",)),
        name="ml_decode",
    )(z, zT, gates, state_c, state_n, state_m, out_gain.astype(F32))
    return act.reshape(rows, ML_V_W), c_new, n_new, m_new[:, 0, :ML_HEADS]


def _subhead_mean_square(t, bsum, split):
    t2 = t * t
    hi = t2.astype(BF16)
    lo = (t2 - hi.astype(F32)).astype(BF16) if split else None
    parts = []
    for j in range(t.shape[1] // V7X_MXU_DIM):
        sl = slice(j * V7X_MXU_DIM, (j + 1) * V7X_MXU_DIM)
        part = jnp.dot(hi[:, sl], bsum, preferred_element_type=F32)
        if split:
            part = part + jnp.dot(lo[:, sl], bsum, preferred_element_type=F32)
        parts.append(part)
    return jnp.concatenate(parts, axis=1)


def _rope(t, cos, sin_signed):
    lane = lax.broadcasted_iota(jnp.int32, (t.shape[0], V7X_LANES), 1)
    first = (lane % DA_HEAD_DIM) < ROT_HALF
    parts = []
    for j in range(t.shape[1] // V7X_LANES):
        tj = t[:, j * V7X_LANES:(j + 1) * V7X_LANES]
        partner = jnp.where(first, pltpu.roll(tj, V7X_LANES - ROT_HALF, axis=1),
                            pltpu.roll(tj, ROT_HALF, axis=1))
        parts.append(tj * cos + partner * sin_signed)
    return jnp.concatenate(parts, axis=1)


def _da_proj_kernel(*refs, for_prompt):
    if for_prompt:
        (x_ref, g_ref, w_ref, qg_ref, kg_ref, bsum_ref, cos_ref, sin_ref,
         wkvT_ref, kgc_ref, cosT_ref, sinT_ref, q_ref, v_ref, kb_ref, kT_ref, vT_ref) = refs
    else:
        (x_ref, g_ref, w_ref, qg_ref, kg_ref, bsum_ref, cos_ref, sin_ref,
         q_ref, k_ref, v_ref) = refs
    h = _rms_rows(x_ref[...], g_ref[...]).astype(BF16)
    z = jnp.dot(h, w_ref[...], preferred_element_type=F32)
    bsum = bsum_ref[...]
    cos = cos_ref[...]
    sin = sin_ref[...]
    split = not for_prompt
    q = z[:, :DA_W]
    q = _rope(q * lax.rsqrt(_subhead_mean_square(q, bsum, split) + EPS) * qg_ref[...], cos, sin)
    q_ref[...] = (q * SCORE_SCALE_LOG2).astype(BF16)
    v_ref[...] = z[:, z.shape[1] - v_ref.shape[1]:]
    if not for_prompt:
        k = z[:, DA_W:2 * DA_W]
        k_ref[...] = _rope(k * lax.rsqrt(_subhead_mean_square(k, bsum, split) + EPS)
                           * kg_ref[...], cos, sin)
        return
    nt = (((1,), (1,)), ((), ()))
    kvT = lax.dot_general(wkvT_ref[...], h, nt, preferred_element_type=F32)
    vT_ref[0] = kvT[DA_W:].astype(BF16)
    tm = kvT.shape[1]
    kT = kvT[:DA_W].reshape(DA_SUB, DA_HEAD_DIM, tm)
    ms = jnp.mean(kT * kT, axis=1, keepdims=True)
    kT = kT * lax.rsqrt(ms + EPS) * kgc_ref[...]
    x1 = kT[:, :ROT_HALF]
    x2 = kT[:, ROT_HALF:ROT_DIM]
    cosT = cosT_ref[...]
    sinT = sinT_ref[...]
    kT = jnp.concatenate([x1 * cosT - x2 * sinT, x2 * cosT + x1 * sinT, kT[:, ROT_DIM:]], axis=1)
    kT = kT.reshape(DA_W, tm)
    kT_ref[0] = kT
    kb_ref[...] = kT.T.astype(BF16)


def _rope_tables(pos):
    inv_freq = jnp.power(ROPE_THETA, -jnp.arange(0, ROT_DIM, 2, dtype=F32) / ROT_DIM)
    ang = pos.astype(F32)[:, None] * inv_freq[None, :]
    cos, sin = jnp.cos(ang), jnp.sin(ang)
    ones = jnp.ones((pos.shape[0], DA_HEAD_DIM - ROT_DIM), F32)
    cos64 = jnp.concatenate([cos, cos, ones], axis=1)
    sin64 = jnp.concatenate([-sin, sin, 0.0 * ones], axis=1)
    return jnp.tile(cos64, (1, 2)), jnp.tile(sin64, (1, 2)), cos.T, sin.T


def _da_proj_call(x, gain, w_qkv, q_gain, k_gain, pos, seq, *, for_prompt, row_target=512):
    rows, d = x.shape
    tm = _row_tile(seq, row_target)
    per_seq = seq // tm
    n_v = w_qkv.shape[1] - 2 * DA_W
    if for_prompt:
        w = jnp.concatenate([w_qkv[:, :DA_W], w_qkv[:, 2 * DA_W:]], axis=1).astype(BF16)
    else:
        w = w_qkv.astype(BF16)
    cos, sin, cosT, sinT = _rope_tables(pos)
    blk = jnp.arange(V7X_MXU_DIM) // DA_HEAD_DIM
    bsum = jnp.where(blk[:, None] == blk[None, :], 1.0 / DA_HEAD_DIM, 0.0).astype(BF16)
    qg = jnp.tile(q_gain.astype(F32), DA_SUB).reshape(1, DA_W)
    kg = jnp.tile(k_gain.astype(F32), DA_SUB).reshape(1, DA_W)
    row = lambda wd: pl.BlockSpec((tm, wd), lambda i: (i, 0))
    table = pl.BlockSpec((tm, V7X_LANES), lambda i: (i % per_seq, 0))
    in_specs = [row(d), _const_spec((1, d)), _const_spec(w.shape), _const_spec((1, DA_W)),
                _const_spec((1, DA_W)), _const_spec(bsum.shape), table, table]
    args = [x, gain.reshape(1, d), w, qg, kg, bsum, cos, sin]
    if for_prompt:
        assert n_v == DA_W
        wkvT = w_qkv[:, DA_W:].T.astype(BF16)
        kgc = k_gain.astype(F32).reshape(1, DA_HEAD_DIM, 1)
        tableT = pl.BlockSpec((ROT_HALF, tm), lambda i: (0, i % per_seq))
        by_seq = pl.BlockSpec((1, DA_W, tm), lambda i: (i // per_seq, 0, i % per_seq))
        in_specs += [_const_spec(wkvT.shape), _const_spec(kgc.shape), tableT, tableT]
        args += [wkvT, kgc, cosT, sinT]
        out_shape = [jax.ShapeDtypeStruct((rows, DA_W), BF16), jax.ShapeDtypeStruct((rows, n_v), F32),
                     jax.ShapeDtypeStruct((rows, DA_W), BF16),
                     jax.ShapeDtypeStruct((rows // seq, DA_W, seq), F32),
                     jax.ShapeDtypeStruct((rows // seq, n_v, seq), BF16)]
        out_specs = [row(DA_W), row(n_v), row(DA_W), by_seq, by_seq]
    else:
        out_shape = [jax.ShapeDtypeStruct((rows, DA_W), BF16),
                     jax.ShapeDtypeStruct((rows, DA_W), F32),
                     jax.ShapeDtypeStruct((rows, n_v), F32)]
        out_specs = [row(DA_W), row(DA_W), row(n_v)]
    return pl.pallas_call(
        functools.partial(_da_proj_kernel, for_prompt=for_prompt),
        out_shape=tuple(out_shape),
        grid=(rows // tm,),
        in_specs=in_specs,
        out_specs=tuple(out_specs),
        compiler_params=_params(("parallel",)),
        name="da_proj_prompt" if for_prompt else "da_proj_decode",
    )(*args)


def _lambda_value(lam_ref, lam_init):
    lp = lam_ref[...]
    s1 = jnp.sum(lp[0:1, :] * lp[1:2, :], axis=1, keepdims=True)
    s2 = jnp.sum(lp[2:3, :] * lp[3:4, :], axis=1, keepdims=True)
    return jnp.exp(s1) - jnp.exp(s2) + lam_init


def _da_attn_kernel(q_ref, k_ref, vT_ref, lam_ref, subln_ref, o_ref, q2_s, m_s, acc_s,
                    *, tq, tk, lam_init):
    i = pl.program_id(1)
    nt = (((1,), (1,)), ((), ()))
    lane = lax.broadcasted_iota(jnp.int32, (tq, V7X_LANES), 1)
    for h in range(DA_HEADS):
        qp = q_ref[0, :, h * DA_VDIM:(h + 1) * DA_VDIM]
        zero = jnp.zeros_like(qp)
        q2_s[h, 0:tq, :] = jnp.where(lane < DA_HEAD_DIM, qp, zero)
        q2_s[h, tq:2 * tq, :] = jnp.where(lane >= DA_HEAD_DIM, qp, zero)
    m_s[...] = jnp.full(m_s.shape, -jnp.inf, F32)
    acc_s[...] = jnp.zeros_like(acc_s)
    ones_rows = jnp.ones((ATTN_SUM_ROWS, tk), BF16)

    key_l = lax.broadcasted_iota(jnp.int32, (tk, 2 * tq), 0)
    qry_g = i * tq + lax.broadcasted_iota(jnp.int32, (tk, 2 * tq), 1) % tq

    def block(j, masked):
        start = pl.multiple_of(j * tk, tk)

        def scores(h):
            kb = k_ref[0, pl.ds(start, tk), h * DA_VDIM:(h + 1) * DA_VDIM]
            return lax.dot_general(kb, q2_s[h], nt, preferred_element_type=F32)

        ahead = [scores(h) for h in range(ATTN_HEADS_AHEAD)]
        for h in range(DA_HEADS):
            hs = slice(h * DA_VDIM, (h + 1) * DA_VDIM)
            s = ahead.pop(0)
            if h + ATTN_HEADS_AHEAD < DA_HEADS:
                ahead.append(scores(h + ATTN_HEADS_AHEAD))
            if masked:
                s = jnp.where(j * tk + key_l <= qry_g, s, -jnp.inf)
            m_old = m_s[h]
            m_new = jnp.maximum(m_old, jnp.max(s, axis=0, keepdims=True))
            alpha = jnp.exp2(m_old - m_new)
            p = jnp.exp2(s - m_new)
            vTb = jnp.concatenate([vT_ref[0, hs, pl.ds(start, tk)], ones_rows], axis=0)
            acc_s[h] = alpha * acc_s[h] + jnp.dot(vTb, p.astype(BF16),
                                                  preferred_element_type=F32)
            m_s[h] = m_new

    n_full = (i * tq) // tk

    def full_block(j, carry):
        block(j, False)
        return carry

    lax.fori_loop(0, n_full, full_block, 0)
    block(n_full, True)

    lam = _lambda_value(lam_ref, lam_init)
    for h in range(DA_HEADS):
        oT = acc_s[h, 0:DA_VDIM] / acc_s[h, DA_VDIM:DA_VDIM + 1]
        oT = oT[:, :tq] - lam * oT[:, tq:]
        ms = jnp.mean(oT * oT, axis=0, keepdims=True)
        oT = oT * lax.rsqrt(ms + EPS) * (subln_ref[...] * (1.0 - lam_init))
        o_ref[0, :, h * DA_VDIM:(h + 1) * DA_VDIM] = oT.T.astype(BF16)


def _da_attn_call(q, kb, vT, lam_p, subln, batch, seq, lam_init):
    tk = _row_tile(seq, ATTN_KV_BLOCK)
    tq = min(_row_tile(seq, ATTN_Q_BLOCK), tk)
    assert tk % tq == 0
    w = q.shape[1]
    q3, k3 = (t.reshape(batch, seq, t.shape[1]) for t in (q, kb))
    out = pl.pallas_call(
        functools.partial(_da_attn_kernel, tq=tq, tk=tk, lam_init=lam_init),
        out_shape=jax.ShapeDtypeStruct((batch, seq, vT.shape[1]), BF16),
        grid=(batch, seq // tq),
        in_specs=[pl.BlockSpec((1, tq, w), lambda b, i: (b, i, 0)),
                  pl.BlockSpec((1, seq, w), lambda b, i: (b, 0, 0)),
                  pl.BlockSpec((1, vT.shape[1], seq), lambda b, i: (b, 0, 0)),
                  pl.BlockSpec(lam_p.shape, lambda b, i: (0, 0)),
                  pl.BlockSpec((DA_VDIM, 1), lambda b, i: (0, 0))],
        out_specs=pl.BlockSpec((1, tq, vT.shape[1]), lambda b, i: (b, i, 0)),
        scratch_shapes=[pltpu.VMEM((DA_HEADS, 2 * tq, DA_VDIM), BF16),
                        pltpu.VMEM((DA_HEADS, 1, 2 * tq), F32),
                        pltpu.VMEM((DA_HEADS, DA_VDIM + ATTN_SUM_ROWS, 2 * tq), F32)],
        compiler_params=_params(("parallel", "arbitrary")),
        name="da_attn_prompt",
    )(q3, k3, vT, lam_p.astype(F32), subln.reshape(DA_VDIM, 1).astype(F32))
    return out.reshape(batch * seq, vT.shape[1])


def _da_decode_kernel(pt_ref, q_ref, kn_ref, vn_ref, lam_ref, subln_ref, expand_ref, *rest,
                      pages, lam_init):
    k_refs = rest[:pages]
    v_refs = rest[pages:2 * pages]
    o_ref, qbd_s, m_s, l_s, acc_s = rest[2 * pages:]
    j = pl.program_id(1)
    flat = (DA_SUB * DA_HEAD_DIM, PAGE_SIZE)

    @pl.when(j == 0)
    def _():
        q = q_ref[0]
        sub = lax.broadcasted_iota(jnp.int32, (DA_SUB, DA_W), 0)
        col = lax.broadcasted_iota(jnp.int32, (DA_SUB, DA_W), 1)
        qb = jnp.broadcast_to(q.astype(F32), (DA_SUB, DA_W))
        qbd_s[...] = jnp.where(col // DA_HEAD_DIM == sub, qb, 0.0).astype(BF16)
        m_s[...] = jnp.full(m_s.shape, -jnp.inf, F32)
        l_s[...] = jnp.zeros_like(l_s)
        acc_s[...] = jnp.zeros_like(acc_s)

    qbd = qbd_s[...]
    s = jnp.concatenate(
        [jnp.dot(qbd, k_refs[i][...].reshape(flat).astype(BF16), preferred_element_type=F32)
         for i in range(pages)], axis=1)
    m_old = m_s[...]
    m_new = jnp.maximum(m_old, jnp.max(s, axis=1, keepdims=True))
    alpha = jnp.exp2(m_old - m_new)
    p = jnp.exp2(s - m_new)
    l_s[...] = alpha * l_s[...] + jnp.sum(p, axis=1, keepdims=True)
    p_rows = jnp.concatenate([p[:, i * PAGE_SIZE:(i + 1) * PAGE_SIZE] for i in range(pages)],
                             axis=0).astype(BF16)
    w = jnp.dot(p_rows, expand_ref[...], preferred_element_type=F32)
    sub = lax.broadcasted_iota(jnp.int32, w.shape, 0) % DA_SUB
    head = lax.broadcasted_iota(jnp.int32, w.shape, 1) % DA_HEADS
    w = jnp.where(head == sub // 2, w, 0.0).astype(BF16)
    pv = jnp.zeros(acc_s.shape, F32)
    for i in range(pages):
        pv += jnp.dot(w[i * DA_SUB:(i + 1) * DA_SUB], v_refs[i][...].reshape(flat).astype(BF16),
                      preferred_element_type=F32)
    acc_s[...] = alpha * acc_s[...] + pv
    m_s[...] = m_new

    @pl.when(j == pl.num_programs(1) - 1)
    def _():
        lam = _lambda_value(lam_ref, lam_init)
        m_p = m_s[...]
        s_self = jnp.sum(qbd.astype(F32) * kn_ref[0], axis=1, keepdims=True)
        m_f = jnp.maximum(m_p, s_self)
        a_f = jnp.exp2(m_p - m_f)
        p_self = jnp.exp2(s_self - m_f)
        l_f = a_f * l_s[...] + p_self
        past = a_f * acc_s[...]
        v_self = vn_ref[0]
        for h in range(DA_HEADS):
            o2 = ((past[2 * h:2 * h + 2] + p_self[2 * h:2 * h + 2] * v_self[h:h + 1])
                  / l_f[2 * h:2 * h + 2])
            oh = o2[0:1] - lam * o2[1:2]
            oh = _rms_rows(oh, subln_ref[...]) * (1.0 - lam_init)
            o_ref[0, h:h + 1, :] = oh.astype(BF16)


def _da_decode_call(q, k_new, v_new, cache_k, cache_v, layer, page_table, lam_p, subln, lam_init):
    rows = q.shape[0]
    n_pages = page_table.shape[1]
    pages = DECODE_PAGES_PER_STEP
    while n_pages % pages:
        pages //= 2
    ck = jnp.transpose(cache_k, (0, 1, 3, 4, 2))
    per_b = lambda a: pl.BlockSpec((1,) + a.shape[1:], lambda b, j, pt: (b, 0, 0))
    page_index = lambda i: (lambda b, j, pt: (layer, pt[b, j * pages + i], 0, 0, 0))
    k_spec = lambda i: pl.BlockSpec((None, None, DA_SUB, DA_HEAD_DIM, PAGE_SIZE), page_index(i))
    v_spec = lambda i: pl.BlockSpec((None, None, PAGE_SIZE, DA_HEADS, DA_VDIM), page_index(i))
    token = jnp.arange(PAGE_SIZE * DA_HEADS) // DA_HEADS
    expand = (token[None, :] == jnp.arange(PAGE_SIZE)[:, None]).astype(BF16)

    q3 = q.reshape(rows, 1, DA_W)
    kn3 = k_new.reshape(rows, 1, DA_W)
    vn3 = v_new.reshape(rows, DA_HEADS, DA_VDIM)
    grid_spec = pltpu.PrefetchScalarGridSpec(
        num_scalar_prefetch=1,
        grid=(rows, n_pages // pages),
        in_specs=[per_b(q3), per_b(kn3), per_b(vn3),
                  pl.BlockSpec(lam_p.shape, lambda b, j, pt: (0, 0)),
                  pl.BlockSpec((1, DA_VDIM), lambda b, j, pt: (0, 0)),
                  pl.BlockSpec(expand.shape, lambda b, j, pt: (0, 0))]
        + [k_spec(i) for i in range(pages)] + [v_spec(i) for i in range(pages)],
        out_specs=pl.BlockSpec((1, DA_HEADS, DA_VDIM), lambda b, j, pt: (b, 0, 0)),
        scratch_shapes=[pltpu.VMEM((DA_SUB, DA_W), BF16), pltpu.VMEM((DA_SUB, 1), F32),
                        pltpu.VMEM((DA_SUB, 1), F32), pltpu.VMEM((DA_SUB, DA_VDIM), F32)],
    )
    out = pl.pallas_call(
        functools.partial(_da_decode_kernel, pages=pages, lam_init=lam_init),
        out_shape=jax.ShapeDtypeStruct((rows, DA_HEADS, DA_VDIM), BF16),
        grid_spec=grid_spec,
        compiler_params=_params(("parallel", "arbitrary")),
        name="da_attn_decode",
    )(page_table, q3, kn3, vn3, lam_p.astype(F32), subln.reshape(1, DA_VDIM).astype(F32), expand,
      *([ck] * pages), *([cache_v] * pages))
    return out.reshape(rows, DA_HEADS * DA_VDIM)


def kernel(x_prompt, x_sample, state_C, state_n, state_m, cache_k, cache_v, page_table,
           ffn1_norm, ffn1_w_gu, ffn1_w_down, mix_norm, ffn2_norm, ffn2_w_gu, ffn2_w_down,
           ml_w_in, ml_gate_bias, ml_out_norm, ml_w_out,
           da_w_qkv, da_q_norm, da_k_norm, da_lambda, da_subln, da_w_out):
    depth = ffn1_norm.shape[0]
    bp, lp, d = x_prompt.shape
    bs, ls, _ = x_sample.shape
    assert ls == 1
    past_len = page_table.shape[1] * PAGE_SIZE
    pos_p = jnp.arange(lp, dtype=jnp.int32)
    pos_s = past_len + jnp.arange(ls, dtype=jnp.int32)

    xp = x_prompt.reshape(bp * lp, d)
    xs = x_sample.reshape(bs * ls, d)
    ml_p, ml_s, kv_p, kv_s = [], [], [], []
    ffn1 = (ffn1_norm.astype(F32).reshape(depth, 1, d), ffn1_w_gu.astype(BF16),
            ffn1_w_down.astype(BF16))
    ffn2 = (ffn2_norm.astype(F32).reshape(depth, 1, d), ffn2_w_gu.astype(BF16),
            ffn2_w_down.astype(BF16))
    for i in range(depth):
        j = i // N_MIXERS
        xp, xs = _ffn_call(xp, xs, *ffn1, i)
        if i % N_MIXERS == 0:
            q, kT, v, o, gates = _ml_proj_prompt_call(xp, mix_norm[i], ml_w_in[j], ml_gate_bias[j],
                                                      bp, lp)
            act_p, c_p, n_p, m_p = _ml_recur_call(q, kT, v, o, gates, ml_out_norm[j], bp, lp)
            ml_p.append((c_p, n_p, m_p))
            z, zT, gts = _ml_proj_decode_call(xs, mix_norm[i], ml_w_in[j], ml_gate_bias[j])
            act_s, c_s, n_s, m_s = _ml_decode_call(z, zT, gts, state_C[j], state_n[j], state_m[j],
                                                   ml_out_norm[j])
            ml_s.append((c_s, n_s, m_s))
            w_out = ml_w_out[j].astype(BF16)
        else:
            lam_init = _lambda_init(i)
            qp, vp, kpb, kpT, vpT = _da_proj_call(xp, mix_norm[i], da_w_qkv[j], da_q_norm[j],
                                                  da_k_norm[j], pos_p, lp, for_prompt=True)
            act_p = _da_attn_call(qp, kpb, vpT, da_lambda[j], da_subln[j], bp, lp, lam_init)
            kp = jnp.transpose(kpT.reshape(bp, DA_SUB, DA_HEAD_DIM, lp), (0, 3, 1, 2))
            kv_p.append((kp, vp.reshape(bp, lp, DA_HEADS, DA_VDIM)))
            qs, ks, vs = _da_proj_call(xs, mix_norm[i], da_w_qkv[j], da_q_norm[j], da_k_norm[j],
                                       jnp.repeat(pos_s, bs), bs, for_prompt=False)
            act_s = _da_decode_call(qs, ks, vs, cache_k, cache_v, j, page_table,
                                    da_lambda[j], da_subln[j], lam_init)
            kv_s.append((ks.reshape(bs, ls, DA_SUB, DA_HEAD_DIM),
                         vs.reshape(bs, ls, DA_HEADS, DA_VDIM)))
            w_out = da_w_out[j].astype(BF16)
        xp, xs = _ffn_call(xp, xs, *ffn2, i, pre=(act_p, act_s, w_out))

    stack = lambda items, k: jnp.stack([it[k] for it in items])
    return (xp.reshape(bp, lp, d), xs.reshape(bs, ls, d),
            stack(ml_p, 0), stack(ml_p, 1), stack(ml_p, 2),
            stack(ml_s, 0), stack(ml_s, 1), stack(ml_s, 2),
            stack(kv_p, 0), stack(kv_p, 1), stack(kv_s, 0), stack(kv_s, 1))
```

```python
import functools
import math

import jax
import jax.numpy as jnp
from jax import lax
from jax.experimental import pallas as pl
from jax.experimental.pallas import tpu as pltpu

F32 = jnp.float32
BF16 = jnp.bfloat16

EPS = 1e-6
ML_HEADS = 4
ML_DK = 128
ML_DV = 256
ML_QK_W = ML_HEADS * ML_DK
ML_V_W = ML_HEADS * ML_DV
GATE_SOFTCAP = 15.0
DA_HEAD_DIM = 64
DA_HEADS = 8
DA_SUB = 2 * DA_HEADS
DA_VDIM = 128
DA_W = DA_SUB * DA_HEAD_DIM
ROPE_THETA = 500000.0
ROT_DIM = DA_HEAD_DIM // 4
ROT_HALF = ROT_DIM // 2
PAGE_SIZE = 128
N_MIXERS = 2
SCORE_SCALE_LOG2 = (DA_HEAD_DIM ** -0.5) * math.log2(math.e)

V7X_LANES = 128
V7X_MXU_DIM = 256
V7X_VMEM_LIMIT_BYTES = 56 * 1024 * 1024

FFN_CHUNK = 256
ML_CHUNK = 256
ATTN_Q_BLOCK = 256
ATTN_KV_BLOCK = 256
ATTN_HEADS_AHEAD = 2
ATTN_SUM_ROWS = 16
DECODE_PAGES_PER_STEP = 16
ML_DECODE_SEQS_PER_STEP = 4


def _lambda_init(layer):
    return 0.8 - 0.6 * math.exp(-0.3 * layer)


def _const_spec(shape):
    zeros = (0,) * len(shape)
    return pl.BlockSpec(shape, lambda *_: zeros, pipeline_mode=pl.Buffered(1))


def _params(semantics):
    return pltpu.CompilerParams(dimension_semantics=semantics,
                                vmem_limit_bytes=V7X_VMEM_LIMIT_BYTES)


def _row_tile(rows, target):
    t = min(rows, target)
    while rows % t:
        t //= 2
    return t


def _rms_rows(x, gain):
    ms = jnp.mean(x * x, axis=-1, keepdims=True)
    return x * lax.rsqrt(ms + EPS) * gain


def _ffn_kernel(*refs, has_pre, d_ff, tf):
    if has_pre:
        x_ref, a_ref, xs_ref, as_ref, wo_ref, g_ref, wgu_ref, wd_ref, o_ref, os_ref, h_scr = refs
    else:
        x_ref, xs_ref, g_ref, wgu_ref, wd_ref, o_ref, os_ref, h_scr = refs
        a_ref = as_ref = None

    def run(x_in, a_in, out):
        rows = x_in.shape[0]
        x = x_in[...]
        if has_pre:
            x = x + jnp.dot(a_in[...], wo_ref[...], preferred_element_type=F32)
        h_scr[0:rows, :] = _rms_rows(x, g_ref[...]).astype(BF16)
        out[...] = x
        for c in range(d_ff // tf):
            h = h_scr[0:rows, :]
            g = jnp.dot(h, wgu_ref[:, c * tf:(c + 1) * tf], preferred_element_type=F32)
            u = jnp.dot(h, wgu_ref[:, d_ff + c * tf:d_ff + (c + 1) * tf],
                        preferred_element_type=F32)
            act = (0.5 * g * jax.nn.sigmoid(g) * u).astype(BF16)
            out[...] += jnp.dot(act, wd_ref[c * tf:(c + 1) * tf, :], preferred_element_type=F32)

    run(x_ref, a_ref, o_ref)

    @pl.when(pl.program_id(0) == pl.num_programs(0) - 1)
    def _():
        run(xs_ref, as_ref, os_ref)


def _ffn_call(x, xs, gains, w_gu, w_down, layer, pre=None, *, row_target=512):
    rows, d = x.shape
    d_ff = w_down.shape[1]
    tf = FFN_CHUNK
    while d_ff % tf:
        tf //= 2
    tm = _row_tile(rows, row_target)
    assert xs.shape[0] <= tm
    row_spec = pl.BlockSpec((tm, d), lambda i: (i, 0))
    whole = lambda a: _const_spec(a.shape)
    of_layer = lambda a: pl.BlockSpec((None,) + a.shape[1:], lambda i: (layer, 0, 0),
                                      pipeline_mode=pl.Buffered(1))
    in_specs, args = [row_spec], [x]
    if pre is not None:
        a, a_s, wo = pre
        in_specs += [pl.BlockSpec((tm, a.shape[1]), lambda i: (i, 0)), whole(xs), whole(a_s),
                     whole(wo)]
        args += [a, xs, a_s, wo]
    else:
        in_specs.append(whole(xs))
        args.append(xs)
    in_specs += [of_layer(gains), of_layer(w_gu), of_layer(w_down)]
    args += [gains, w_gu, w_down]
    return pl.pallas_call(
        functools.partial(_ffn_kernel, has_pre=pre is not None, d_ff=d_ff, tf=tf),
        out_shape=(jax.ShapeDtypeStruct((rows, d), F32), jax.ShapeDtypeStruct(xs.shape, F32)),
        grid=(rows // tm,),
        in_specs=in_specs,
        out_specs=(row_spec, pl.BlockSpec(xs.shape, lambda i: (0, 0))),
        scratch_shapes=[pltpu.VMEM((tm, d), BF16)],
        compiler_params=_params(("arbitrary",)),
        name="ffn_pre" if pre is not None else "ffn",
    )(*args)


def _activate_gates(g, is_forget):
    g = GATE_SOFTCAP * jnp.tanh(g / GATE_SOFTCAP)
    log_sig = jnp.minimum(g, 0.0) - jnp.log1p(jnp.exp(-jnp.abs(g)))
    return jnp.where(is_forget, log_sig, g)


def _ml_proj_prompt_kernel(x_ref, g_ref, wqvo_ref, wkT_ref, wgT_ref, gb_ref,
                           q_ref, kT_ref, v_ref, o_ref, gates_ref):
    h = _rms_rows(x_ref[...], g_ref[...]).astype(BF16)
    z = jnp.dot(h, wqvo_ref[...], preferred_element_type=F32)
    q_ref[...] = z[:, :ML_QK_W].astype(BF16)
    v_ref[...] = z[:, ML_QK_W:ML_QK_W + ML_V_W].astype(BF16)
    o_ref[...] = z[:, ML_QK_W + ML_V_W:]
    nt = (((1,), (1,)), ((), ()))
    kT = lax.dot_general(wkT_ref[...], h, nt, preferred_element_type=F32)
    kT_ref[0] = (kT * (ML_DK ** -0.5)).astype(BF16)
    gT = lax.dot_general(wgT_ref[...], h, nt, preferred_element_type=F32) + gb_ref[...]
    row = lax.broadcasted_iota(jnp.int32, gT.shape, 0)
    gates_ref[0] = _activate_gates(gT, row >= ML_HEADS)


def _ml_proj_prompt_call(x, gain, w_in, gate_bias, batch, seq, *, row_target=512):
    rows, d = x.shape
    tm = _row_tile(seq, row_target)
    per_seq = seq // tm
    ng = 2 * ML_HEADS
    wqvo = jnp.concatenate([w_in[:, :ML_QK_W], w_in[:, 2 * ML_QK_W:2 * ML_QK_W + 2 * ML_V_W]],
                           axis=1).astype(BF16)
    wkT = w_in[:, ML_QK_W:2 * ML_QK_W].T.astype(BF16)
    wgT = w_in[:, 2 * ML_QK_W + 2 * ML_V_W:].T.astype(BF16)
    row = lambda w: pl.BlockSpec((tm, w), lambda i: (i, 0))
    by_seq = lambda r: pl.BlockSpec((1, r, tm), lambda i: (i // per_seq, 0, i % per_seq))
    return pl.pallas_call(
        _ml_proj_prompt_kernel,
        out_shape=(jax.ShapeDtypeStruct((rows, ML_QK_W), BF16),
                   jax.ShapeDtypeStruct((batch, ML_QK_W, seq), BF16),
                   jax.ShapeDtypeStruct((rows, ML_V_W), BF16),
                   jax.ShapeDtypeStruct((rows, ML_V_W), F32),
                   jax.ShapeDtypeStruct((batch, ng, seq), F32)),
        grid=(rows // tm,),
        in_specs=[row(d), _const_spec((1, d)), _const_spec(wqvo.shape), _const_spec(wkT.shape),
                  _const_spec(wgT.shape), _const_spec((ng, 1))],
        out_specs=(row(ML_QK_W), by_seq(ML_QK_W), row(ML_V_W), row(ML_V_W), by_seq(ng)),
        compiler_params=_params(("parallel",)),
        name="ml_proj_prompt",
    )(x, gain.reshape(1, d), wqvo, wkT, wgT, gate_bias.reshape(ng, 1).astype(F32))


def _ml_recur_kernel(q_ref, kT_ref, v_ref, o_ref, gates_ref, gain_ref,
                     act_ref, c_out, n_out, m_out, c_s, n_s, m_s, *, chunk):
    j = pl.program_id(1)

    @pl.when(j == 0)
    def _():
        c_s[...] = jnp.zeros_like(c_s)
        n_s[...] = jnp.zeros_like(n_s)
        m_s[...] = jnp.zeros_like(m_s)

    gates = gates_ref[0]
    lane = lax.broadcasted_iota(jnp.int32, gates.shape, 1)
    cum = gates
    step = 1
    while step < chunk:
        cum = cum + jnp.where(lane >= step, pltpu.roll(cum, step, axis=1), 0.0)
        step *= 2
    t_idx = lax.broadcasted_iota(jnp.int32, (chunk, chunk), 0)
    s_idx = lax.broadcasted_iota(jnp.int32, (chunk, chunk), 1)
    causal = s_idx <= t_idx
    nt = (((1,), (1,)), ((), ()))

    heads = range(ML_HEADS)
    dot = functools.partial(jnp.dot, preferred_element_type=F32)
    qh = [q_ref[0, :, h * ML_DK:(h + 1) * ML_DK] for h in heads]
    kTh = [kT_ref[0, h * ML_DK:(h + 1) * ML_DK, :] for h in heads]
    vh = [v_ref[0, :, h * ML_DV:(h + 1) * ML_DV] for h in heads]
    c_old = [c_s[h] for h in heads]
    n_row = [n_s[h:h + 1, :] for h in heads]
    m_prev = [m_s[h:h + 1, 0:1] for h in heads]
    qk = [dot(qh[h], kTh[h]) for h in heads]
    qc = [dot(qh[h], c_old[h].astype(BF16)) for h in heads]
    qn = [jnp.sum(qh[h].astype(F32) * n_row[h], axis=1, keepdims=True) for h in heads]

    li_r = [gates[h:h + 1, :] for h in heads]
    lf_r = [gates[ML_HEADS + h:ML_HEADS + h + 1, :] for h in heads]
    b_r = [cum[ML_HEADS + h:ML_HEADS + h + 1, :] for h in heads]
    b_last = [b[:, chunk - 1:chunk] for b in b_r]
    b_c = [jnp.sum(jnp.where(causal, lf, 0.0), axis=1, keepdims=True) for lf in lf_r]
    d_mat = [jnp.where(causal, b_c[h] - b_r[h] + li_r[h], -jnp.inf) for h in heads]
    a = [b_c[h] + m_prev[h] for h in heads]
    mt = [jnp.maximum(a[h], jnp.max(d_mat[h], axis=1, keepdims=True)) for h in heads]
    s = [qk[h] * jnp.exp(d_mat[h] - mt[h]) for h in heads]
    sv = [dot(s[h].astype(BF16), vh[h]) for h in heads]

    g_r = [b_last[h] - b_r[h] + li_r[h] for h in heads]
    m_new = [jnp.maximum(b_last[h] + m_prev[h], jnp.max(g_r[h], axis=1, keepdims=True))
             for h in heads]
    decay = [jnp.exp(b_last[h] + m_prev[h] - m_new[h]) for h in heads]
    wk = [jnp.exp(g_r[h] - m_new[h]) for h in heads]
    kv = [dot((kTh[h].astype(F32) * wk[h]).astype(BF16), vh[h]) for h in heads]
    nk = [lax.dot_general(wk[h].astype(BF16), kTh[h], nt, preferred_element_type=F32)
          for h in heads]

    for h in heads:
        w_inter = jnp.exp(a[h] - mt[h])
        num = w_inter * qc[h] + sv[h]
        den = w_inter * qn[h] + jnp.sum(s[h], axis=1, keepdims=True)
        hid = num / jnp.maximum(jnp.abs(den), jnp.exp(-mt[h]))
        hn = _rms_rows(hid, gain_ref[h:h + 1, :])
        og = jax.nn.sigmoid(o_ref[0, :, h * ML_DV:(h + 1) * ML_DV])
        act_ref[0, :, h * ML_DV:(h + 1) * ML_DV] = (hn * og).astype(BF16)
    for h in heads:
        c_s[h] = decay[h] * c_old[h] + kv[h]
        n_s[h:h + 1, :] = decay[h] * n_row[h] + nk[h]
        m_s[h:h + 1, :] = jnp.broadcast_to(m_new[h], (1, V7X_LANES))

    @pl.when(j == pl.num_programs(1) - 1)
    def _():
        c_out[0] = c_s[...]
        n_out[0] = n_s[0:ML_HEADS, :]
        lane_m = lax.broadcasted_iota(jnp.int32, (1, V7X_LANES), 1)
        m_row = jnp.zeros((1, V7X_LANES), F32)
        for h in range(ML_HEADS):
            m_row = jnp.where(lane_m == h, m_s[h:h + 1, :], m_row)
        m_out[0] = m_row


def _ml_recur_call(q, kT, v, o, gates, out_gain, batch, seq):
    chunk = _row_tile(seq, ML_CHUNK)
    n_steps = seq // chunk
    ng = 2 * ML_HEADS
    q3 = q.reshape(batch, seq, ML_QK_W)
    v3 = v.reshape(batch, seq, ML_V_W)
    o3 = o.reshape(batch, seq, ML_V_W)
    tok = lambda w: pl.BlockSpec((1, chunk, w), lambda b, j: (b, j, 0))
    col = lambda r: pl.BlockSpec((1, r, chunk), lambda b, j: (b, 0, j))
    per_b = lambda *s: pl.BlockSpec((1,) + s, lambda b, j: (b,) + (0,) * len(s))
    act, c_new, n_new, m_new = pl.pallas_call(
        functools.partial(_ml_recur_kernel, chunk=chunk),
        out_shape=(jax.ShapeDtypeStruct((batch, seq, ML_V_W), BF16),
                   jax.ShapeDtypeStruct((batch, ML_HEADS, ML_DK, ML_DV), F32),
                   jax.ShapeDtypeStruct((batch, ML_HEADS, ML_DK), F32),
                   jax.ShapeDtypeStruct((batch, 1, V7X_LANES), F32)),
        grid=(batch, n_steps),
        in_specs=[tok(ML_QK_W), col(ML_QK_W), tok(ML_V_W), tok(ML_V_W), col(ng),
                  pl.BlockSpec((ML_HEADS, ML_DV), lambda b, j: (0, 0))],
        out_specs=(tok(ML_V_W), per_b(ML_HEADS, ML_DK, ML_DV), per_b(ML_HEADS, ML_DK),
                   per_b(1, V7X_LANES)),
        scratch_shapes=[pltpu.VMEM((ML_HEADS, ML_DK, ML_DV), F32),
                        pltpu.VMEM((8, ML_DK), F32),
                        pltpu.VMEM((8, V7X_LANES), F32)],
        compiler_params=_params(("parallel", "arbitrary")),
        name="ml_recur",
    )(q3, kT, v3, o3, gates, out_gain.astype(F32))
    return act.reshape(batch * seq, ML_V_W), c_new, n_new, m_new[:, 0, :ML_HEADS]


def _ml_proj_decode_kernel(x_ref, g_ref, w_ref, wqkT_ref, wg_ref, gb_ref, z_ref, zT_ref, gates_ref):
    h = _rms_rows(x_ref[...], g_ref[...]).astype(BF16)
    z = jnp.dot(h, w_ref[...], preferred_element_type=F32)
    col = lax.broadcasted_iota(jnp.int32, z.shape, 1)
    is_k = (col >= ML_QK_W) & (col < 2 * ML_QK_W)
    z_ref[...] = jnp.where(is_k, z * (ML_DK ** -0.5), z)
    nt = (((1,), (1,)), ((), ()))
    zT = lax.dot_general(wqkT_ref[...], h, nt, preferred_element_type=F32)
    rowi = lax.broadcasted_iota(jnp.int32, zT.shape, 0)
    zT_ref[...] = jnp.where(rowi >= ML_QK_W, zT * (ML_DK ** -0.5), zT)
    g = jnp.dot(h, wg_ref[...], preferred_element_type=F32) + gb_ref[...]
    lane = lax.broadcasted_iota(jnp.int32, g.shape, 1)
    gates_ref[...] = _activate_gates(g, lane >= ML_HEADS)


def _ml_proj_decode_call(x, gain, w_in, gate_bias):
    rows, d = x.shape
    ng = 2 * ML_HEADS
    w_main = w_in[:, :2 * ML_QK_W + 2 * ML_V_W].astype(BF16)
    wqkT = w_in[:, :2 * ML_QK_W].T.astype(BF16)
    wg = jnp.pad(w_in[:, 2 * ML_QK_W + 2 * ML_V_W:], ((0, 0), (0, V7X_LANES - ng))).astype(BF16)
    gb = jnp.pad(gate_bias.astype(F32), (0, V7X_LANES - ng)).reshape(1, V7X_LANES)
    full = lambda a: _const_spec(a.shape)
    args = (x, gain.reshape(1, d), w_main, wqkT, wg, gb)
    out_shape = (jax.ShapeDtypeStruct((rows, w_main.shape[1]), F32),
                 jax.ShapeDtypeStruct((2 * ML_QK_W, rows), F32),
                 jax.ShapeDtypeStruct((rows, V7X_LANES), F32))
    return pl.pallas_call(
        _ml_proj_decode_kernel,
        out_shape=out_shape,
        grid=(1,),
        in_specs=[full(a) for a in args],
        out_specs=tuple(pl.BlockSpec(s.shape, lambda i: (0, 0)) for s in out_shape),
        compiler_params=_params(("arbitrary",)),
        name="ml_proj_decode",
    )(*args)


def _ml_decode_kernel(z_ref, zT_ref, gates_ref, c_ref, n_ref, m_ref, gain_ref,
                      act_ref, c_out, n_out, m_out, *, seqs):
    for r in range(seqs):
        _ml_decode_one(pl.program_id(0) * seqs + r, r, z_ref, zT_ref, gates_ref, c_ref, n_ref,
                       m_ref, gain_ref, act_ref, c_out, n_out, m_out)


def _ml_decode_one(b, r, z_ref, zT_ref, gates_ref, c_ref, n_ref, m_ref, gain_ref,
                   act_ref, c_out, n_out, m_out):
    z = z_ref[pl.ds(b, 1), :]
    g = gates_ref[pl.ds(b, 1), :]
    m_in = m_ref[pl.ds(b, 1), :]
    zT = zT_ref[...]
    pick = lax.broadcasted_iota(jnp.int32, zT.shape, 1) == b
    z_col = jnp.sum(jnp.where(pick, zT, 0.0), axis=1, keepdims=True)
    lane_m = lax.broadcasted_iota(jnp.int32, (1, V7X_LANES), 1)
    m_row = jnp.zeros((1, V7X_LANES), F32)

    for h in range(ML_HEADS):
        li = g[:, h:h + 1]
        lf = g[:, ML_HEADS + h:ML_HEADS + h + 1]
        m_prev = m_in[:, h:h + 1]
        q_row = z[:, h * ML_DK:(h + 1) * ML_DK]
        k_row = z[:, ML_QK_W + h * ML_DK:ML_QK_W + (h + 1) * ML_DK]
        v_row = z[:, 2 * ML_QK_W + h * ML_DV:2 * ML_QK_W + (h + 1) * ML_DV]
        o_row = z[:, 2 * ML_QK_W + ML_V_W + h * ML_DV:2 * ML_QK_W + ML_V_W + (h + 1) * ML_DV]
        q_col = z_col[h * ML_DK:(h + 1) * ML_DK, :]
        k_col = z_col[ML_QK_W + h * ML_DK:ML_QK_W + (h + 1) * ML_DK, :]
        c_old = c_ref[r, h]
        n_row = n_ref[r, h:h + 1, :]

        a = lf + m_prev
        mt = jnp.maximum(a, li)
        w_inter = jnp.exp(a - mt)
        s = jnp.sum(q_row * k_row, axis=1, keepdims=True) * jnp.exp(li - mt)
        num = w_inter * jnp.sum(q_col * c_old, axis=0, keepdims=True) + s * v_row
        den = w_inter * jnp.sum(q_row * n_row, axis=1, keepdims=True) + s
        hid = num / jnp.maximum(jnp.abs(den), jnp.exp(-mt))
        hn = _rms_rows(hid, gain_ref[h:h + 1, :])
        act_ref[r, :, h * ML_DV:(h + 1) * ML_DV] = (hn * jax.nn.sigmoid(o_row)).astype(BF16)

        m_new = jnp.maximum(a, li)
        decay = jnp.exp(a - m_new)
        wk = jnp.exp(li - m_new)
        c_out[r, h] = decay * c_old + (wk * k_col) * v_row
        n_out[r, h:h + 1, :] = decay * n_row + wk * k_row
        m_row = jnp.where(lane_m == h, m_new, m_row)
    m_out[r] = m_row


def _ml_decode_call(z, zT, gates, state_c, state_n, state_m, out_gain):
    rows = z.shape[0]
    seqs = _row_tile(rows, ML_DECODE_SEQS_PER_STEP)
    full = lambda a: _const_spec(a.shape)
    per_b = lambda *s: pl.BlockSpec((seqs,) + s, lambda b: (b,) + (0,) * len(s))
    act, c_new, n_new, m_new = pl.pallas_call(
        functools.partial(_ml_decode_kernel, seqs=seqs),
        out_shape=(jax.ShapeDtypeStruct((rows, 1, ML_V_W), BF16),
                   jax.ShapeDtypeStruct(state_c.shape, F32),
                   jax.ShapeDtypeStruct(state_n.shape, F32),
                   jax.ShapeDtypeStruct((rows, 1, V7X_LANES), F32)),
        grid=(rows // seqs,),
        in_specs=[full(z), full(zT), full(gates), per_b(ML_HEADS, ML_DK, ML_DV),
                  per_b(ML_HEADS, ML_DK), full(state_m), full(out_gain)],
        out_specs=(per_b(1, ML_V_W), per_b(ML_HEADS, ML_DK, ML_DV), per_b(ML_HEADS, ML_DK),
                   per_b(1, V7X_LANES)),
        compiler_params=_params(("arbitrary",)),
        name="ml_decode",
    )(z, zT, gates, state_c, state_n, state_m, out_gain.astype(F32))
    return act.reshape(rows, ML_V_W), c_new, n_new, m_new[:, 0, :ML_HEADS]


def _subhead_mean_square(t, bsum, split):
    t2 = t * t
    hi = t2.astype(BF16)
    lo = (t2 - hi.astype(F32)).astype(BF16) if split else None
    parts = []
    for j in range(t.shape[1] // V7X_MXU_DIM):
        sl = slice(j * V7X_MXU_DIM, (j + 1) * V7X_MXU_DIM)
        part = jnp.dot(hi[:, sl], bsum, preferred_element_type=F32)
        if split:
            part = part + jnp.dot(lo[:, sl], bsum, preferred_element_type=F32)
        parts.append(part)
    return jnp.concatenate(parts, axis=1)


def _rope(t, cos, sin_signed):
    lane = lax.broadcasted_iota(jnp.int32, (t.shape[0], V7X_LANES), 1)
    first = (lane % DA_HEAD_DIM) < ROT_HALF
    parts = []
    for j in range(t.shape[1] // V7X_LANES):
        tj = t[:, j * V7X_LANES:(j + 1) * V7X_LANES]
        partner = jnp.where(first, pltpu.roll(tj, V7X_LANES - ROT_HALF, axis=1),
                            pltpu.roll(tj, ROT_HALF, axis=1))
        parts.append(tj * cos + partner * sin_signed)
    return jnp.concatenate(parts, axis=1)


def _da_proj_kernel(*refs, for_prompt):
    if for_prompt:
        (x_ref, g_ref, w_ref, qg_ref, kg_ref, bsum_ref, cos_ref, sin_ref,
         wkvT_ref, kgc_ref, cosT_ref, sinT_ref, q_ref, v_ref, kb_ref, kT_ref, vT_ref) = refs
    else:
        (x_ref, g_ref, w_ref, qg_ref, kg_ref, bsum_ref, cos_ref, sin_ref,
         q_ref, k_ref, v_ref) = refs
    h = _rms_rows(x_ref[...], g_ref[...]).astype(BF16)
    z = jnp.dot(h, w_ref[...], preferred_element_type=F32)
    bsum = bsum_ref[...]
    cos = cos_ref[...]
    sin = sin_ref[...]
    split = not for_prompt
    q = z[:, :DA_W]
    q = _rope(q * lax.rsqrt(_subhead_mean_square(q, bsum, split) + EPS) * qg_ref[...], cos, sin)
    q_ref[...] = (q * SCORE_SCALE_LOG2).astype(BF16)
    v_ref[...] = z[:, z.shape[1] - v_ref.shape[1]:]
    if not for_prompt:
        k = z[:, DA_W:2 * DA_W]
        k_ref[...] = _rope(k * lax.rsqrt(_subhead_mean_square(k, bsum, split) + EPS)
                           * kg_ref[...], cos, sin)
        return
    nt = (((1,), (1,)), ((), ()))
    kvT = lax.dot_general(wkvT_ref[...], h, nt, preferred_element_type=F32)
    vT_ref[0] = kvT[DA_W:].astype(BF16)
    tm = kvT.shape[1]
    kT = kvT[:DA_W].reshape(DA_SUB, DA_HEAD_DIM, tm)
    ms = jnp.mean(kT * kT, axis=1, keepdims=True)
    kT = kT * lax.rsqrt(ms + EPS) * kgc_ref[...]
    x1 = kT[:, :ROT_HALF]
    x2 = kT[:, ROT_HALF:ROT_DIM]
    cosT = cosT_ref[...]
    sinT = sinT_ref[...]
    kT = jnp.concatenate([x1 * cosT - x2 * sinT, x2 * cosT + x1 * sinT, kT[:, ROT_DIM:]], axis=1)
    kT = kT.reshape(DA_W, tm)
    kT_ref[0] = kT
    kb_ref[...] = kT.T.astype(BF16)


def _rope_tables(pos):
    inv_freq = jnp.power(ROPE_THETA, -jnp.arange(0, ROT_DIM, 2, dtype=F32) / ROT_DIM)
    ang = pos.astype(F32)[:, None] * inv_freq[None, :]
    cos, sin = jnp.cos(ang), jnp.sin(ang)
    ones = jnp.ones((pos.shape[0], DA_HEAD_DIM - ROT_DIM), F32)
    cos64 = jnp.concatenate([cos, cos, ones], axis=1)
    sin64 = jnp.concatenate([-sin, sin, 0.0 * ones], axis=1)
    return jnp.tile(cos64, (1, 2)), jnp.tile(sin64, (1, 2)), cos.T, sin.T


def _da_proj_call(x, gain, w_qkv, q_gain, k_gain, pos, seq, *, for_prompt, row_target=512):
    rows, d = x.shape
    tm = _row_tile(seq, row_target)
    per_seq = seq // tm
    n_v = w_qkv.shape[1] - 2 * DA_W
    if for_prompt:
        w = jnp.concatenate([w_qkv[:, :DA_W], w_qkv[:, 2 * DA_W:]], axis=1).astype(BF16)
    else:
        w = w_qkv.astype(BF16)
    cos, sin, cosT, sinT = _rope_tables(pos)
    blk = jnp.arange(V7X_MXU_DIM) // DA_HEAD_DIM
    bsum = jnp.where(blk[:, None] == blk[None, :], 1.0 / DA_HEAD_DIM, 0.0).astype(BF16)
    qg = jnp.tile(q_gain.astype(F32), DA_SUB).reshape(1, DA_W)
    kg = jnp.tile(k_gain.astype(F32), DA_SUB).reshape(1, DA_W)
    row = lambda wd: pl.BlockSpec((tm, wd), lambda i: (i, 0))
    table = pl.BlockSpec((tm, V7X_LANES), lambda i: (i % per_seq, 0))
    in_specs = [row(d), _const_spec((1, d)), _const_spec(w.shape), _const_spec((1, DA_W)),
                _const_spec((1, DA_W)), _const_spec(bsum.shape), table, table]
    args = [x, gain.reshape(1, d), w, qg, kg, bsum, cos, sin]
    if for_prompt:
        assert n_v == DA_W
        wkvT = w_qkv[:, DA_W:].T.astype(BF16)
        kgc = k_gain.astype(F32).reshape(1, DA_HEAD_DIM, 1)
        tableT = pl.BlockSpec((ROT_HALF, tm), lambda i: (0, i % per_seq))
        by_seq = pl.BlockSpec((1, DA_W, tm), lambda i: (i // per_seq, 0, i % per_seq))
        in_specs += [_const_spec(wkvT.shape), _const_spec(kgc.shape), tableT, tableT]
        args += [wkvT, kgc, cosT, sinT]
        out_shape = [jax.ShapeDtypeStruct((rows, DA_W), BF16), jax.ShapeDtypeStruct((rows, n_v), F32),
                     jax.ShapeDtypeStruct((rows, DA_W), BF16),
                     jax.ShapeDtypeStruct((rows // seq, DA_W, seq), F32),
                     jax.ShapeDtypeStruct((rows // seq, n_v, seq), BF16)]
        out_specs = [row(DA_W), row(n_v), row(DA_W), by_seq, by_seq]
    else:
        out_shape = [jax.ShapeDtypeStruct((rows, DA_W), BF16),
                     jax.ShapeDtypeStruct((rows, DA_W), F32),
                     jax.ShapeDtypeStruct((rows, n_v), F32)]
        out_specs = [row(DA_W), row(DA_W), row(n_v)]
    return pl.pallas_call(
        functools.partial(_da_proj_kernel, for_prompt=for_prompt),
        out_shape=tuple(out_shape),
        grid=(rows // tm,),
        in_specs=in_specs,
        out_specs=tuple(out_specs),
        compiler_params=_params(("parallel",)),
        name="da_proj_prompt" if for_prompt else "da_proj_decode",
    )(*args)


def _lambda_value(lam_ref, lam_init):
    lp = lam_ref[...]
    s1 = jnp.sum(lp[0:1, :] * lp[1:2, :], axis=1, keepdims=True)
    s2 = jnp.sum(lp[2:3, :] * lp[3:4, :], axis=1, keepdims=True)
    return jnp.exp(s1) - jnp.exp(s2) + lam_init


def _da_attn_kernel(q_ref, k_ref, vT_ref, lam_ref, subln_ref, o_ref, q2_s, m_s, acc_s,
                    *, tq, tk, lam_init):
    i = pl.program_id(1)
    nt = (((1,), (1,)), ((), ()))
    lane = lax.broadcasted_iota(jnp.int32, (tq, V7X_LANES), 1)
    for h in range(DA_HEADS):
        qp = q_ref[0, :, h * DA_VDIM:(h + 1) * DA_VDIM]
        zero = jnp.zeros_like(qp)
        q2_s[h, 0:tq, :] = jnp.where(lane < DA_HEAD_DIM, qp, zero)
        q2_s[h, tq:2 * tq, :] = jnp.where(lane >= DA_HEAD_DIM, qp, zero)
    m_s[...] = jnp.full(m_s.shape, -jnp.inf, F32)
    acc_s[...] = jnp.zeros_like(acc_s)
    ones_rows = jnp.ones((ATTN_SUM_ROWS, tk), BF16)

    key_l = lax.broadcasted_iota(jnp.int32, (tk, 2 * tq), 0)
    qry_g = i * tq + lax.broadcasted_iota(jnp.int32, (tk, 2 * tq), 1) % tq

    def block(j, masked):
        start = pl.multiple_of(j * tk, tk)

        def scores(h):
            kb = k_ref[0, pl.ds(start, tk), h * DA_VDIM:(h + 1) * DA_VDIM]
            return lax.dot_general(kb, q2_s[h], nt, preferred_element_type=F32)

        ahead = [scores(h) for h in range(ATTN_HEADS_AHEAD)]
        for h in range(DA_HEADS):
            hs = slice(h * DA_VDIM, (h + 1) * DA_VDIM)
            s = ahead.pop(0)
            if h + ATTN_HEADS_AHEAD < DA_HEADS:
                ahead.append(scores(h + ATTN_HEADS_AHEAD))
            if masked:
                s = jnp.where(j * tk + key_l <= qry_g, s, -jnp.inf)
            m_old = m_s[h]
            m_new = jnp.maximum(m_old, jnp.max(s, axis=0, keepdims=True))
            alpha = jnp.exp2(m_old - m_new)
            p = jnp.exp2(s - m_new)
            vTb = jnp.concatenate([vT_ref[0, hs, pl.ds(start, tk)], ones_rows], axis=0)
            acc_s[h] = alpha * acc_s[h] + jnp.dot(vTb, p.astype(BF16),
                                                  preferred_element_type=F32)
            m_s[h] = m_new

    n_full = (i * tq) // tk

    def full_block(j, carry):
        block(j, False)
        return carry

    lax.fori_loop(0, n_full, full_block, 0)
    block(n_full, True)

    lam = _lambda_value(lam_ref, lam_init)
    for h in range(DA_HEADS):
        oT = acc_s[h, 0:DA_VDIM] / acc_s[h, DA_VDIM:DA_VDIM + 1]
        oT = oT[:, :tq] - lam * oT[:, tq:]
        ms = jnp.mean(oT * oT, axis=0, keepdims=True)
        oT = oT * lax.rsqrt(ms + EPS) * (subln_ref[...] * (1.0 - lam_init))
        o_ref[0, :, h * DA_VDIM:(h + 1) * DA_VDIM] = oT.T.astype(BF16)


def _da_attn_call(q, kb, vT, lam_p, subln, batch, seq, lam_init):
    tk = _row_tile(seq, ATTN_KV_BLOCK)
    tq = min(_row_tile(seq, ATTN_Q_BLOCK), tk)
    assert tk % tq == 0
    w = q.shape[1]
    q3, k3 = (t.reshape(batch, seq, t.shape[1]) for t in (q, kb))
    out = pl.pallas_call(
        functools.partial(_da_attn_kernel, tq=tq, tk=tk, lam_init=lam_init),
        out_shape=jax.ShapeDtypeStruct((batch, seq, vT.shape[1]), BF16),
        grid=(batch, seq // tq),
        in_specs=[pl.BlockSpec((1, tq, w), lambda b, i: (b, i, 0)),
                  pl.BlockSpec((1, seq, w), lambda b, i: (b, 0, 0)),
                  pl.BlockSpec((1, vT.shape[1], seq), lambda b, i: (b, 0, 0)),
                  pl.BlockSpec(lam_p.shape, lambda b, i: (0, 0)),
                  pl.BlockSpec((DA_VDIM, 1), lambda b, i: (0, 0))],
        out_specs=pl.BlockSpec((1, tq, vT.shape[1]), lambda b, i: (b, i, 0)),
        scratch_shapes=[pltpu.VMEM((DA_HEADS, 2 * tq, DA_VDIM), BF16),
                        pltpu.VMEM((DA_HEADS, 1, 2 * tq), F32),
                        pltpu.VMEM((DA_HEADS, DA_VDIM + ATTN_SUM_ROWS, 2 * tq), F32)],
        compiler_params=_params(("parallel", "arbitrary")),
        name="da_attn_prompt",
    )(q3, k3, vT, lam_p.astype(F32), subln.reshape(DA_VDIM, 1).astype(F32))
    return out.reshape(batch * seq, vT.shape[1])


def _da_decode_kernel(pt_ref, q_ref, kn_ref, vn_ref, lam_ref, subln_ref, expand_ref, *rest,
                      pages, lam_init):
    k_refs = rest[:pages]
    v_refs = rest[pages:2 * pages]
    o_ref, qbd_s, m_s, l_s, acc_s = rest[2 * pages:]
    j = pl.program_id(1)
    flat = (DA_SUB * DA_HEAD_DIM, PAGE_SIZE)

    @pl.when(j == 0)
    def _():
        q = q_ref[0]
        sub = lax.broadcasted_iota(jnp.int32, (DA_SUB, DA_W), 0)
        col = lax.broadcasted_iota(jnp.int32, (DA_SUB, DA_W), 1)
        qb = jnp.broadcast_to(q.astype(F32), (DA_SUB, DA_W))
        qbd_s[...] = jnp.where(col // DA_HEAD_DIM == sub, qb, 0.0).astype(BF16)
        m_s[...] = jnp.full(m_s.shape, -jnp.inf, F32)
        l_s[...] = jnp.zeros_like(l_s)
        acc_s[...] = jnp.zeros_like(acc_s)

    qbd = qbd_s[...]
    s = jnp.concatenate(
        [jnp.dot(qbd, k_refs[i][...].reshape(flat).astype(BF16), preferred_element_type=F32)
         for i in range(pages)], axis=1)
    m_old = m_s[...]
    m_new = jnp.maximum(m_old, jnp.max(s, axis=1, keepdims=True))
    alpha = jnp.exp2(m_old - m_new)
    p = jnp.exp2(s - m_new)
    l_s[...] = alpha * l_s[...] + jnp.sum(p, axis=1, keepdims=True)
    p_rows = jnp.concatenate([p[:, i * PAGE_SIZE:(i + 1) * PAGE_SIZE] for i in range(pages)],
                             axis=0).astype(BF16)
    w = jnp.dot(p_rows, expand_ref[...], preferred_element_type=F32)
    sub = lax.broadcasted_iota(jnp.int32, w.shape, 0) % DA_SUB
    head = lax.broadcasted_iota(jnp.int32, w.shape, 1) % DA_HEADS
    w = jnp.where(head == sub // 2, w, 0.0).astype(BF16)
    pv = jnp.zeros(acc_s.shape, F32)
    for i in range(pages):
        pv += jnp.dot(w[i * DA_SUB:(i + 1) * DA_SUB], v_refs[i][...].reshape(flat).astype(BF16),
                      preferred_element_type=F32)
    acc_s[...] = alpha * acc_s[...] + pv
    m_s[...] = m_new

    @pl.when(j == pl.num_programs(1) - 1)
    def _():
        lam = _lambda_value(lam_ref, lam_init)
        m_p = m_s[...]
        s_self = jnp.sum(qbd.astype(F32) * kn_ref[0], axis=1, keepdims=True)
        m_f = jnp.maximum(m_p, s_self)
        a_f = jnp.exp2(m_p - m_f)
        p_self = jnp.exp2(s_self - m_f)
        l_f = a_f * l_s[...] + p_self
        past = a_f * acc_s[...]
        v_self = vn_ref[0]
        for h in range(DA_HEADS):
            o2 = ((past[2 * h:2 * h + 2] + p_self[2 * h:2 * h + 2] * v_self[h:h + 1])
                  / l_f[2 * h:2 * h + 2])
            oh = o2[0:1] - lam * o2[1:2]
            oh = _rms_rows(oh, subln_ref[...]) * (1.0 - lam_init)
            o_ref[0, h:h + 1, :] = oh.astype(BF16)


def _da_decode_call(q, k_new, v_new, cache_k, cache_v, layer, page_table, lam_p, subln, lam_init):
    rows = q.shape[0]
    n_pages = page_table.shape[1]
    pages = DECODE_PAGES_PER_STEP
    while n_pages % pages:
        pages //= 2
    ck = jnp.transpose(cache_k, (0, 1, 3, 4, 2))
    per_b = lambda a: pl.BlockSpec((1,) + a.shape[1:], lambda b, j, pt: (b, 0, 0))
    page_index = lambda i: (lambda b, j, pt: (layer, pt[b, j * pages + i], 0, 0, 0))
    k_spec = lambda i: pl.BlockSpec((None, None, DA_SUB, DA_HEAD_DIM, PAGE_SIZE), page_index(i))
    v_spec = lambda i: pl.BlockSpec((None, None, PAGE_SIZE, DA_HEADS, DA_VDIM), page_index(i))
    token = jnp.arange(PAGE_SIZE * DA_HEADS) // DA_HEADS
    expand = (token[None, :] == jnp.arange(PAGE_SIZE)[:, None]).astype(BF16)

    q3 = q.reshape(rows, 1, DA_W)
    kn3 = k_new.reshape(rows, 1, DA_W)
    vn3 = v_new.reshape(rows, DA_HEADS, DA_VDIM)
    grid_spec = pltpu.PrefetchScalarGridSpec(
        num_scalar_prefetch=1,
        grid=(rows, n_pages // pages),
        in_specs=[per_b(q3), per_b(kn3), per_b(vn3),
                  pl.BlockSpec(lam_p.shape, lambda b, j, pt: (0, 0)),
                  pl.BlockSpec((1, DA_VDIM), lambda b, j, pt: (0, 0)),
                  pl.BlockSpec(expand.shape, lambda b, j, pt: (0, 0))]
        + [k_spec(i) for i in range(pages)] + [v_spec(i) for i in range(pages)],
        out_specs=pl.BlockSpec((1, DA_HEADS, DA_VDIM), lambda b, j, pt: (b, 0, 0)),
        scratch_shapes=[pltpu.VMEM((DA_SUB, DA_W), BF16), pltpu.VMEM((DA_SUB, 1), F32),
                        pltpu.VMEM((DA_SUB, 1), F32), pltpu.VMEM((DA_SUB, DA_VDIM), F32)],
    )
    out = pl.pallas_call(
        functools.partial(_da_decode_kernel, pages=pages, lam_init=lam_init),
        out_shape=jax.ShapeDtypeStruct((rows, DA_HEADS, DA_VDIM), BF16),
        grid_spec=grid_spec,
        compiler_params=_params(("parallel", "arbitrary")),
        name="da_attn_decode",
    )(page_table, q3, kn3, vn3, lam_p.astype(F32), subln.reshape(1, DA_VDIM).astype(F32), expand,
      *([ck] * pages), *([cache_v] * pages))
    return out.reshape(rows, DA_HEADS * DA_VDIM)


def kernel(x_prompt, x_sample, state_C, state_n, state_m, cache_k, cache_v, page_table,
           ffn1_norm, ffn1_w_gu, ffn1_w_down, mix_norm, ffn2_norm, ffn2_w_gu, ffn2_w_down,
           ml_w_in, ml_gate_bias, ml_out_norm, ml_w_out,
           da_w_qkv, da_q_norm, da_k_norm, da_lambda, da_subln, da_w_out):
    depth = ffn1_norm.shape[0]
    bp, lp, d = x_prompt.shape
    bs, ls, _ = x_sample.shape
    assert ls == 1
    past_len = page_table.shape[1] * PAGE_SIZE
    pos_p = jnp.arange(lp, dtype=jnp.int32)
    pos_s = past_len + jnp.arange(ls, dtype=jnp.int32)

    xp = x_prompt.reshape(bp * lp, d)
    xs = x_sample.reshape(bs * ls, d)
    ml_p, ml_s, kv_p, kv_s = [], [], [], []
    ffn1 = (ffn1_norm.astype(F32).reshape(depth, 1, d), ffn1_w_gu.astype(BF16),
            ffn1_w_down.astype(BF16))
    ffn2 = (ffn2_norm.astype(F32).reshape(depth, 1, d), ffn2_w_gu.astype(BF16),
            ffn2_w_down.astype(BF16))
    for i in range(depth):
        j = i // N_MIXERS
        xp, xs = _ffn_call(xp, xs, *ffn1, i)
        if i % N_MIXERS == 0:
            q, kT, v, o, gates = _ml_proj_prompt_call(xp, mix_norm[i], ml_w_in[j], ml_gate_bias[j],
                                                      bp, lp)
            act_p, c_p, n_p, m_p = _ml_recur_call(q, kT, v, o, gates, ml_out_norm[j], bp, lp)
            ml_p.append((c_p, n_p, m_p))
            z, zT, gts = _ml_proj_decode_call(xs, mix_norm[i], ml_w_in[j], ml_gate_bias[j])
            act_s, c_s, n_s, m_s = _ml_decode_call(z, zT, gts, state_C[j], state_n[j], state_m[j],
                                                   ml_out_norm[j])
            ml_s.append((c_s, n_s, m_s))
            w_out = ml_w_out[j].astype(BF16)
        else:
            lam_init = _lambda_init(i)
            qp, vp, kpb, kpT, vpT = _da_proj_call(xp, mix_norm[i], da_w_qkv[j], da_q_norm[j],
                                                  da_k_norm[j], pos_p, lp, for_prompt=True)
            act_p = _da_attn_call(qp, kpb, vpT, da_lambda[j], da_subln[j], bp, lp, lam_init)
            kp = jnp.transpose(kpT.reshape(bp, DA_SUB, DA_HEAD_DIM, lp), (0, 3, 1, 2))
            kv_p.append((kp, vp.reshape(bp, lp, DA_HEADS, DA_VDIM)))
            qs, ks, vs = _da_proj_call(xs, mix_norm[i], da_w_qkv[j], da_q_norm[j], da_k_norm[j],
                                       jnp.repeat(pos_s, bs), bs, for_prompt=False)
            act_s = _da_decode_call(qs, ks, vs, cache_k, cache_v, j, page_table,
                                    da_lambda[j], da_subln[j], lam_init)
            kv_s.append((ks.reshape(bs, ls, DA_SUB, DA_HEAD_DIM),
                         vs.reshape(bs, ls, DA_HEADS, DA_VDIM)))
            w_out = da_w_out[j].astype(BF16)
        xp, xs = _ffn_call(xp, xs, *ffn2, i, pre=(act_p, act_s, w_out))

    stack = lambda items, k: jnp.stack([it[k] for it in items])
    return (xp.reshape(bp, lp, d), xs.reshape(bs, ls, d),
            stack(ml_p, 0), stack(ml_p, 1), stack(ml_p, 2),
            stack(ml_s, 0), stack(ml_s, 1), stack(ml_s, 2),
            stack(kv_p, 0), stack(kv_p, 1), stack(kv_s, 0), stack(kv_s, 1))
```
